```python
import math
import jax, jax.numpy as jnp
from jax import lax
import numpy as np

D_MODEL = 2048
BATCH = 4
SEQ = 2048
DEPTH = 2

HG_HEADS = 8
HG_KDIM = 128
HG_VDIM = 128
HG_WIDTH = HG_HEADS * HG_VDIM
HG_CHUNK = 64
POOL_WINDOWS = (2, 4, 8, 16)
POOL_GROUPS = len(POOL_WINDOWS)
POOL_WIDTH = 1024
POOL_GDIM = POOL_WIDTH // POOL_GROUPS
SG_GROUPS = 8
SG_WIDTH = 1024
SG_GDIM = SG_WIDTH // SG_GROUPS
SG_CHUNK = 128
N_BRANCH = 3
D_FF = 5632
CONV_W = 3
LN_EPS = 1e-5
RMS_EPS = 1e-6
DN_ALPHA = (2 * DEPTH) ** 0.25
DN_BETA = (8 * DEPTH) ** -0.25
IN_WIDTHS = (HG_HEADS * HG_KDIM, HG_HEADS * HG_KDIM, HG_WIDTH, HG_WIDTH,
             POOL_WIDTH, SG_WIDTH, SG_WIDTH, N_BRANCH * D_MODEL)
D_IN = sum(IN_WIDTHS)
IN_SPLITS = tuple(int(c) for c in np.cumsum(IN_WIDTHS)[:-1])

kernel_name = "hybrid_hgrn2_pool_sgu_deepnorm"


def layer_norm(x, g, b):
    xf = x.astype(jnp.float32)
    mu = jnp.mean(xf, axis=-1, keepdims=True)
    var = jnp.mean(jnp.square(xf - mu), axis=-1, keepdims=True)
    y = (xf - mu) * lax.rsqrt(var + LN_EPS) * g.astype(jnp.float32) + b.astype(jnp.float32)
    return y.astype(x.dtype)


def hgrn2_mixer(q, f_raw, v, og, lb, norm_g):
    B, S, _ = v.shape
    dt = v.dtype
    nc = S // HG_CHUNK
    f32 = jnp.float32
    lbf = lb.astype(f32)
    fr = f_raw.astype(f32)
    qf = jax.nn.silu(q.astype(f32))
    log_f = jnp.logaddexp(jnp.log(lbf), jnp.log1p(-lbf) + jax.nn.log_sigmoid(fr))
    kf = (1.0 - lbf) * jax.nn.sigmoid(-fr)
    vf = v.astype(f32)

    def to_chunks(t, d):
        return t.reshape(B, nc, HG_CHUNK, HG_HEADS, d).transpose(1, 0, 3, 2, 4)

    qc, kc, lfc = to_chunks(qf, HG_KDIM), to_chunks(kf, HG_KDIM), to_chunks(log_f, HG_KDIM)
    vc = to_chunks(vf, HG_VDIM)
    causal = jnp.tril(jnp.ones((HG_CHUNK, HG_CHUNK), dtype=bool))

    def step(state, inp):
        q_c, k_c, lf_c, v_c = inp
        b = jnp.cumsum(lf_c, axis=2)
        b_last = b[:, :, -1:, :]
        o_inter = jnp.einsum('bhck,bhkv->bhcv', q_c * jnp.exp(b), state)
        diff = b[:, :, :, None, :] - b[:, :, None, :, :]
        decay = jnp.exp(jnp.where(causal[:, :, None], diff, -jnp.inf))
        scores = jnp.einsum('bhtk,bhtsk,bhsk->bhts', q_c, decay, k_c)
        o = o_inter + jnp.einsum('bhts,bhsv->bhtv', scores, v_c)
        new_state = (jnp.exp(b_last[:, :, 0, :])[..., None] * state
                     + jnp.einsum('bhsk,bhsv->bhkv', k_c * jnp.exp(b_last - b), v_c))
        return new_state, o

    s0 = jnp.zeros((B, HG_HEADS, HG_KDIM, HG_VDIM), f32)
    _, o = lax.scan(step, s0, (qc, kc, lfc, vc))
    o = o.transpose(1, 0, 3, 2, 4).reshape(B, S, HG_HEADS, HG_VDIM)
    o = o * lax.rsqrt(jnp.mean(jnp.square(o), axis=-1, keepdims=True) + RMS_EPS)
    o = o.reshape(B, S, HG_WIDTH) * norm_g.astype(f32) * jax.nn.silu(og.astype(f32))
    return o.astype(dt)


def pool_mixer(p, w_grp, scale):
    B, S, _ = p.shape
    dt = p.dtype
    pf = p.astype(jnp.float32).reshape(B, S, POOL_GROUPS, POOL_GDIM)
    cs = jnp.cumsum(pf, axis=1)
    count_base = jnp.arange(1, S + 1, dtype=jnp.float32)
    outs = []
    for g, w in enumerate(POOL_WINDOWS):
        c = cs[:, :, g]
        lagged = jnp.pad(c, ((0, 0), (w, 0), (0, 0)))[:, :S]
        mean = (c - lagged) / jnp.minimum(count_base, float(w))[None, :, None]
        outs.append(mean - pf[:, :, g])
    pooled = jnp.stack(outs, axis=2)
    y = jnp.einsum('bsgc,gcd->bsgd', pooled, w_grp.astype(jnp.float32)).reshape(B, S, POOL_WIDTH)
    return (y * scale.astype(jnp.float32)).astype(dt)


def sgu_mixer(u, v, ln_g, ln_b, w_s, b_s):
    B, S, _ = u.shape
    u = jax.nn.gelu(u)
    v = layer_norm(jax.nn.gelu(v), ln_g, ln_b)
    nc = S // SG_CHUNK
    vc = v.reshape(B, nc, SG_CHUNK, SG_GROUPS, SG_GDIM)
    w = w_s * jnp.tril(jnp.ones((SG_CHUNK, SG_CHUNK), dtype=w_s.dtype))
    mixed = jnp.einsum('gts,bnsgd->bntgd', w, vc) + b_s.T[None, None, :, :, None]
    return u * mixed.reshape(B, S, SG_WIDTH)


def conv_ffn(x, w_up, conv_w, conv_b, w_down):
    h = x @ w_up
    S = h.shape[1]
    hp = jnp.pad(h, ((0, 0), (CONV_W - 1, 0), (0, 0)))
    acc = conv_b + conv_w[0] * hp[:, 0:S]
    for j in range(1, CONV_W):
        acc = acc + conv_w[j] * hp[:, j:j + S]
    a, b = jnp.split(acc, 2, axis=-1)
    return (jax.nn.silu(a) * b) @ w_down


def setup_inputs(seed: int = 0) -> dict:
    key = jax.random.key(seed)
    ks = jax.random.split(key, 24)
    L = DEPTH

    def nrm(k, shape, scale):
        return jax.random.normal(k, shape, jnp.float32) * scale

    return {
        "x": nrm(ks[0], (BATCH, SEQ, D_MODEL), 1.0),
        "w_in": nrm(ks[1], (L, D_MODEL, D_IN), D_MODEL ** -0.5),
        "hg_lower_bounds": nrm(ks[2], (L, HG_HEADS * HG_KDIM), 0.1),
        "hg_norm_g": 1.0 + nrm(ks[3], (L, HG_WIDTH), 0.02),
        "pool_w": nrm(ks[4], (L, POOL_GROUPS, POOL_GDIM, POOL_GDIM), POOL_GDIM ** -0.5),
        "pool_scale": 1.0 + nrm(ks[5], (L, POOL_WIDTH), 0.02),
        "sg_ln_g": 1.0 + nrm(ks[6], (L, SG_WIDTH), 0.02),
        "sg_ln_b": nrm(ks[7], (L, SG_WIDTH), 0.02),
        "sg_w": nrm(ks[8], (L, SG_GROUPS, SG_CHUNK, SG_CHUNK), 0.5 * SG_CHUNK ** -0.5),
        "sg_b": 1.0 + nrm(ks[9], (L, SG_GROUPS, SG_CHUNK), 0.02),
        "w_hg_proj": nrm(ks[10], (L, HG_WIDTH, D_MODEL), DN_BETA * HG_WIDTH ** -0.5),
        "w_pool_proj": nrm(ks[11], (L, POOL_WIDTH, D_MODEL), DN_BETA * POOL_WIDTH ** -0.5),
        "w_sg_proj": nrm(ks[12], (L, SG_WIDTH, D_MODEL), DN_BETA * SG_WIDTH ** -0.5),
        "w_out": nrm(ks[13], (L, D_MODEL, D_MODEL), DN_BETA * D_MODEL ** -0.5),
        "ln1_g": 1.0 + nrm(ks[14], (L, D_MODEL), 0.02),
        "ln1_b": nrm(ks[15], (L, D_MODEL), 0.02),
        "w_up": nrm(ks[16], (L, D_MODEL, 2 * D_FF), DN_BETA * D_MODEL ** -0.5),
        "conv_w": nrm(ks[17], (L, CONV_W, 2 * D_FF), CONV_W ** -0.5),
        "conv_b": nrm(ks[18], (L, 2 * D_FF), 0.02),
        "w_down": nrm(ks[19], (L, D_FF, D_MODEL), DN_BETA * D_FF ** -0.5),
        "ln2_g": 1.0 + nrm(ks[20], (L, D_MODEL), 0.02),
        "ln2_b": nrm(ks[21], (L, D_MODEL), 0.02),
    }


def reference(x, w_in, hg_lower_bounds, hg_norm_g, pool_w, pool_scale, sg_ln_g, sg_ln_b,
              sg_w, sg_b, w_hg_proj, w_pool_proj, w_sg_proj, w_out, ln1_g, ln1_b,
              w_up, conv_w, conv_b, w_down, ln2_g, ln2_b):
    B, S, D = x.shape
    lb_all = jnp.cumsum(jax.nn.softmax(hg_lower_bounds.astype(jnp.float32), axis=0), axis=0)
    lb_all = lb_all - lb_all[0:1]
    for l in range(DEPTH):
        z = x @ w_in[l]
        q, f_raw, i_in, og, p, u, v, gr = jnp.split(z, IN_SPLITS, axis=-1)
        y_hg = hgrn2_mixer(q, f_raw, i_in, og, lb_all[l], hg_norm_g[l])
        y_pool = pool_mixer(p, pool_w[l], pool_scale[l])
        y_sg = sgu_mixer(u, v, sg_ln_g[l], sg_ln_b[l], sg_w[l], sg_b[l])
        gates = jax.nn.sigmoid(gr.astype(jnp.float32)).astype(x.dtype).reshape(B, S, N_BRANCH, D)
        merged = (gates[:, :, 0] * (y_hg @ w_hg_proj[l])
                  + gates[:, :, 1] * (y_pool @ w_pool_proj[l])
                  + gates[:, :, 2] * (y_sg @ w_sg_proj[l]))
        mix = merged @ w_out[l]
        x = layer_norm(DN_ALPHA * x + mix, ln1_g[l], ln1_b[l])
        x = layer_norm(DN_ALPHA * x + conv_ffn(x, w_up[l], conv_w[l], conv_b[l], w_down[l]),
                       ln2_g[l], ln2_b[l])
    return x
```

```python
import functools
import math

import jax
import jax.numpy as jnp
from jax import lax
from jax.experimental import pallas as pl
from jax.experimental.pallas import tpu as pltpu

F32 = jnp.float32
BF16 = jnp.bfloat16

HG_HEADS = 8
HG_DIM = 128
POOL_WINDOWS = (2, 4, 8, 16)
SG_GROUPS = 8
SG_CHUNK = 128
N_BRANCH = 3
CONV_W = 3
LN_EPS = 1e-5
RMS_EPS = 1e-6

LANES = 128
SUBLANES = 8
BF16_ROWS = 16
VMEM_LIMIT_BYTES = 56 * 1024 * 1024

HG_PAIR = 2 * HG_DIM
HG_CHUNK = 128
HG_LEVELS = tuple(HG_CHUNK >> (i + 1) for i in range(int(math.log2(HG_CHUNK))))
POOL_HALO = 16
TOKEN_TILE = 512
MERGE_TN = 512
FFN_TF = 512


def _params(semantics):
    return pltpu.CompilerParams(dimension_semantics=semantics, vmem_limit_bytes=VMEM_LIMIT_BYTES)


def _dot(a, b):
    return jnp.dot(a, b, preferred_element_type=F32)


def _dot_nt(a, b):
    return lax.dot_general(a, b, (((1,), (1,)), ((), ())), preferred_element_type=F32)


def _dot_tn(a, b):
    return lax.dot_general(a, b, (((0,), (0,)), ((), ())), preferred_element_type=F32)


def _sigmoid(x):
    return 1.0 / (1.0 + jnp.exp(-x))


def _silu(x):
    return x * _sigmoid(x)


def _gelu_tanh(x):
    c = math.sqrt(2.0 / math.pi)
    return 0.5 * x * (1.0 + jnp.tanh(c * (x + 0.044715 * (x * x * x))))


def _layer_norm(y, g, b):
    mu = jnp.mean(y, axis=-1, keepdims=True)
    d = y - mu
    var = jnp.mean(d * d, axis=-1, keepdims=True)
    return d * lax.rsqrt(var + LN_EPS) * g + b


def _block_reference(b, h, row):
    t, c = b.shape
    if 2 * h >= SUBLANES:
        blk = b.reshape(t // (2 * h), 2 * h, c)
        ref = jnp.broadcast_to(blk[:, h - 1:h, :], blk.shape)
        return ref.reshape(t, c)
    pos = row & (2 * h - 1)
    out = b
    for p in range(2 * h):
        off = p - (h - 1)
        if off == 0:
            continue
        shifted = pltpu.roll(b, off % t, axis=0)
        out = jnp.where(pos == p, shifted, out)
    return out


def _hgrn_kernel(layer, x_ref, wq_ref, wf_ref, wi_ref, wg_ref, lbp_ref, ng_ref, o_ref, state_ref):
    @pl.when(pl.program_id(2) == 0)
    def _():
        state_ref[...] = jnp.zeros_like(state_ref)

    x = x_ref[0]
    t = x.shape[0]
    zq = _dot(x, wq_ref[...])
    zf = _dot(x, wf_ref[...])
    zi = _dot(x, wi_ref[...])
    zg = _dot(x, wg_ref[...])

    qf = _silu(zq)
    e = jnp.exp(-jnp.abs(zf))
    inv1pe = 1.0 / (1.0 + e)
    log_sig = jnp.minimum(zf, 0.0) - jnp.log(1.0 + e)
    sig_neg = jnp.where(zf >= 0.0, e * inv1pe, inv1pe)
    if layer == 0:
        lf = log_sig
        kf = sig_neg
    else:
        p = lbp_ref[...]
        rows = [p[i:i + 1, :] for i in range(p.shape[0])]
        m = functools.reduce(jnp.maximum, rows)
        ex = [jnp.exp(r - m) for r in rows]
        lb = functools.reduce(jnp.add, ex[1:layer + 1]) / functools.reduce(jnp.add, ex)
        la = jnp.log(lb)
        lbv = jnp.log(1.0 - lb) + log_sig
        lf = jnp.maximum(la, lbv) + jnp.log(1.0 + jnp.exp(-jnp.abs(la - lbv)))
        kf = (1.0 - lb) * sig_neg

    row = lax.broadcasted_iota(jnp.int32, (t, 1), 0)
    pos = row & (HG_CHUNK - 1)
    b = lf
    sh = 1
    while sh < HG_CHUNK:
        b = b + jnp.where(pos >= sh, pltpu.roll(b, sh, axis=0), 0.0)
        sh *= 2

    n_chunks = t // HG_CHUNK
    n_heads = HG_PAIR // HG_DIM
    ti = lax.broadcasted_iota(jnp.int32, (HG_CHUNK, HG_CHUNK), 0)
    si = lax.broadcasted_iota(jnp.int32, (HG_CHUNK, HG_CHUNK), 1)

    scores = [[None] * n_heads for _ in range(n_chunks)]
    for h in HG_LEVELS:
        ref = _block_reference(b, h, row)
        w = jnp.exp(-jnp.abs(b - ref))
        second = (row & h) != 0
        qt = jnp.where(second, qf * w, 0.0).astype(BF16)
        kt = jnp.where(second, 0.0, kf * w).astype(BF16)
        same = (ti // (2 * h)) == (si // (2 * h))
        for c in range(n_chunks):
            r0 = c * HG_CHUNK
            for hh in range(n_heads):
                c0 = hh * HG_DIM
                a = _dot_nt(qt[r0:r0 + HG_CHUNK, c0:c0 + HG_DIM], kt[r0:r0 + HG_CHUNK, c0:c0 + HG_DIM])
                a = jnp.where(same, a, 0.0)
                scores[c][hh] = a if scores[c][hh] is None else scores[c][hh] + a

    vb = zi.astype(BF16)
    qk = qf * kf
    qd = (qf * jnp.exp(b)).astype(BF16)
    ng = ng_ref[...]
    gate = _silu(zg)
    for hh in range(n_heads):
        c0 = hh * HG_DIM
        st = state_ref[hh]
        for c in range(n_chunks):
            r0 = c * HG_CHUNK
            rs = slice(r0, r0 + HG_CHUNK)
            cs = slice(c0, c0 + HG_DIM)
            b_c = b[rs, cs]
            b_last = b_c[HG_CHUNK - 1:HG_CHUNK, :]
            v_c = vb[rs, cs]
            o = _dot(scores[c][hh].astype(BF16), v_c)
            o = o + jnp.sum(qk[rs, cs], axis=-1, keepdims=True) * zi[rs, cs]
            o = o + _dot_nt(qd[rs, cs], st.astype(BF16))
            kd = (kf[rs, cs] * jnp.exp(b_last - b_c)).astype(BF16)
            st = jnp.exp(b_last) * st + _dot_tn(v_c, kd)
            o = o * lax.rsqrt(jnp.mean(o * o, axis=-1, keepdims=True) + RMS_EPS)
            o = o * ng[:, cs] * gate[rs, cs]
            o_ref[0, rs, cs] = o.astype(o_ref.dtype)
        state_ref[hh] = st


def _hgrn(layer, xb, w_in, lbp, ng):
    bsz, seq, d = xb.shape
    width = HG_HEADS * HG_DIM
    n_pairs = width // HG_PAIR
    ts = min(TOKEN_TILE, seq)
    wspec = lambda k: pl.BlockSpec((d, HG_PAIR), lambda b, p, s, k=k: (0, k * n_pairs + p))
    return pl.pallas_call(
        functools.partial(_hgrn_kernel, layer),
        grid=(bsz, n_pairs, seq // ts),
        in_specs=[
            pl.BlockSpec((1, ts, d), lambda b, p, s: (b, s, 0)),
            wspec(0), wspec(1), wspec(2), wspec(3),
            pl.BlockSpec((lbp.shape[0], HG_PAIR), lambda b, p, s: (0, p)),
            pl.BlockSpec((1, HG_PAIR), lambda b, p, s: (0, p)),
        ],
        out_specs=pl.BlockSpec((1, ts, HG_PAIR), lambda b, p, s: (b, s, p)),
        out_shape=jax.ShapeDtypeStruct((bsz, seq, width), BF16),
        scratch_shapes=[pltpu.VMEM((HG_PAIR // HG_DIM, HG_DIM, HG_DIM), F32)],
        compiler_params=_params(("parallel", "parallel", "arbitrary")),
        name="hgrn",
    )(xb, w_in, w_in, w_in, w_in, lbp, ng)


def _pool_kernel(x_ref, wp_ref, pw_ref, sc_ref, o_ref, halo_ref):
    s = pl.program_id(1)

    @pl.when(s == 0)
    def _():
        halo_ref[...] = jnp.zeros_like(halo_ref)

    x = x_ref[0]
    t = x.shape[0]
    p = _dot(x, wp_ref[...])
    ext = jnp.concatenate([halo_ref[...], p], axis=0)
    halo_ref[...] = p[t - POOL_HALO:, :]
    avail = s * t + lax.broadcasted_iota(jnp.int32, (t, 1), 0) + 1
    gdim = pw_ref.shape[1]
    for g, w in enumerate(POOL_WINDOWS):
        cs = slice(g * gdim, (g + 1) * gdim)
        acc = ext[:, cs]
        sh = 1
        while sh < w:
            acc = acc + pltpu.roll(acc, sh, axis=0)
            sh *= 2
        cnt = jnp.minimum(avail, w).astype(F32)
        pooled = acc[POOL_HALO:, :] * (1.0 / cnt) - p[:, cs]
        y = _dot(pooled.astype(BF16), pw_ref[g]) * sc_ref[:, cs]
        o_ref[0, :, cs] = y.astype(o_ref.dtype)


def _pool(xb, w_in, col_block, pw, sc):
    bsz, seq, d = xb.shape
    g, gdim, _ = pw.shape
    width = g * gdim
    ts = min(TOKEN_TILE, seq)
    return pl.pallas_call(
        _pool_kernel,
        grid=(bsz, seq // ts),
        in_specs=[
            pl.BlockSpec((1, ts, d), lambda b, s: (b, s, 0)),
            pl.BlockSpec((d, width), lambda b, s: (0, col_block)),
            pl.BlockSpec((g, gdim, gdim), lambda b, s: (0, 0, 0)),
            pl.BlockSpec((1, width), lambda b, s: (0, 0)),
        ],
        out_specs=pl.BlockSpec((1, ts, width), lambda b, s: (b, s, 0)),
        out_shape=jax.ShapeDtypeStruct((bsz, seq, width), BF16),
        scratch_shapes=[pltpu.VMEM((POOL_HALO, width), F32)],
        compiler_params=_params(("parallel", "arbitrary")),
        name="pool",
    )(xb, w_in, pw, sc)


def _sgu_kernel(x_ref, wu_ref, wv_ref, g_ref, b_ref, ws_ref, bias_ref, o_ref):
    x = x_ref[...]
    t = x.shape[0]
    gu = _gelu_tanh(_dot(x, wu_ref[...]))
    vn = _layer_norm(_gelu_tanh(_dot(x, wv_ref[...])), g_ref[...], b_ref[...]).astype(BF16)
    n_chunks = t // SG_CHUNK
    gdim = vn.shape[1] // SG_GROUPS
    ri = lax.broadcasted_iota(jnp.int32, (SG_CHUNK, SG_CHUNK), 0)
    ci = lax.broadcasted_iota(jnp.int32, (SG_CHUNK, SG_CHUNK), 1)
    for g in range(SG_GROUPS):
        cs = slice(g * gdim, (g + 1) * gdim)
        w = jnp.where(ri >= ci, ws_ref[g], jnp.zeros((), BF16))
        rhs = jnp.concatenate([vn[c * SG_CHUNK:(c + 1) * SG_CHUNK, cs] for c in range(n_chunks)], axis=1)
        mixed = _dot(w, rhs)
        for c in range(n_chunks):
            rs = slice(c * SG_CHUNK, (c + 1) * SG_CHUNK)
            y = gu[rs, cs] * (mixed[:, c * gdim:(c + 1) * gdim] + bias_ref[:, cs])
            o_ref[rs, cs] = y.astype(o_ref.dtype)


def _sgu(xb2, w_in, u_block, v_block, g, b, ws, bias):
    n, d = xb2.shape
    width = g.shape[1]
    ts = min(TOKEN_TILE, n)
    const2 = lambda i: (0, 0)
    return pl.pallas_call(
        _sgu_kernel,
        grid=(n // ts,),
        in_specs=[
            pl.BlockSpec((ts, d), lambda i: (i, 0)),
            pl.BlockSpec((d, width), lambda i: (0, u_block)),
            pl.BlockSpec((d, width), lambda i: (0, v_block)),
            pl.BlockSpec((1, width), const2),
            pl.BlockSpec((1, width), const2),
            pl.BlockSpec(ws.shape, lambda i: (0, 0, 0)),
            pl.BlockSpec(bias.shape, const2),
        ],
        out_specs=pl.BlockSpec((ts, width), lambda i: (i, 0)),
        out_shape=jax.ShapeDtypeStruct((n, width), BF16),
        compiler_params=_params(("parallel",)),
        name="sgu",
    )(xb2, w_in, w_in, g, b, ws, bias)


def _merge_kernel(x_ref, yh_ref, yp_ref, ys_ref, g0_ref, g1_ref, g2_ref, p0_ref, p1_ref, p2_ref, o_ref):
    x = x_ref[...]
    acc = None
    for y_ref, gw_ref, pw_ref in ((yh_ref, g0_ref, p0_ref), (yp_ref, g1_ref, p1_ref), (ys_ref, g2_ref, p2_ref)):
        term = _sigmoid(_dot(x, gw_ref[...])) * _dot(y_ref[...], pw_ref[...])
        acc = term if acc is None else acc + term
    o_ref[...] = acc.astype(o_ref.dtype)


def _merge(xb2, yh, yp, ys, w_in, gate_col0, ph, pp, ps):
    n, d = xb2.shape
    tm = min(TOKEN_TILE, n)
    tn = MERGE_TN
    gblk0 = gate_col0 // tn
    per_branch = d // tn
    yspec = lambda y: pl.BlockSpec((tm, y.shape[1]), lambda i, j: (i, 0))
    gspec = lambda k: pl.BlockSpec((d, tn), lambda i, j, k=k: (0, gblk0 + k * per_branch + j))
    pspec = lambda w: pl.BlockSpec((w.shape[0], tn), lambda i, j: (0, j))
    return pl.pallas_call(
        _merge_kernel,
        grid=(n // tm, d // tn),
        in_specs=[pl.BlockSpec((tm, d), lambda i, j: (i, 0)), yspec(yh), yspec(yp), yspec(ys),
                  gspec(0), gspec(1), gspec(2), pspec(ph), pspec(pp), pspec(ps)],
        out_specs=pl.BlockSpec((tm, tn), lambda i, j: (i, j)),
        out_shape=jax.ShapeDtypeStruct((n, d), BF16),
        compiler_params=_params(("parallel", "arbitrary")),
        name="merge",
    )(xb2, yh, yp, ys, w_in, w_in, w_in, ph, pp, ps)


def _outln_kernel(alpha, m_ref, w_ref, x_ref, g_ref, b_ref, o_ref, ob_ref):
    y = alpha * x_ref[...] + _dot(m_ref[...], w_ref[...])
    y = _layer_norm(y, g_ref[...], b_ref[...])
    o_ref[...] = y
    ob_ref[...] = y.astype(ob_ref.dtype)


def _outln(alpha, merged, w_out, x2, g, b):
    n, d = x2.shape
    tm = min(TOKEN_TILE // 2, n)
    const2 = lambda i: (0, 0)
    row = pl.BlockSpec((tm, d), lambda i: (i, 0))
    return pl.pallas_call(
        functools.partial(_outln_kernel, alpha),
        grid=(n // tm,),
        in_specs=[row, pl.BlockSpec((d, d), const2), row, pl.BlockSpec((1, d), const2), pl.BlockSpec((1, d), const2)],
        out_specs=[row, row],
        out_shape=[jax.ShapeDtypeStruct((n, d), F32), jax.ShapeDtypeStruct((n, d), BF16)],
        compiler_params=_params(("parallel",)),
        name="outln",
    )(merged, w_out, x2, g, b)


def _causal_conv(h, halo, cw, cb, row):
    n = halo.shape[0]
    h1 = jnp.where(row == 0, halo[n - 1:n, :], pltpu.roll(h, 1, axis=0))
    h2 = jnp.where(row == 0, halo[n - 2:n - 1, :], jnp.where(row == 1, halo[n - 1:n, :], pltpu.roll(h, 2, axis=0)))
    return cb + cw[0:1, :] * h2 + cw[1:2, :] * h1 + cw[2:3, :] * h


def _ffn_kernel(alpha, tiles_per_seq, xb_ref, xh_ref, wa_ref, wb_ref, cwa_ref, cwb_ref, cba_ref, cbb_ref,
                wd_ref, x_ref, g_ref, b_ref, o_ref, ob_ref, acc_ref):
    i = pl.program_id(0)
    j = pl.program_id(1)

    @pl.when(j == 0)
    def _():
        acc_ref[...] = jnp.zeros_like(acc_ref)

    xb = xb_ref[...]
    xh = jnp.where(i % tiles_per_seq == 0, jnp.zeros_like(xh_ref), xh_ref[...])
    row = lax.broadcasted_iota(jnp.int32, (xb.shape[0], 1), 0)
    wa = wa_ref[...]
    wb = wb_ref[...]
    a = _causal_conv(_dot(xb, wa), _dot(xh, wa), cwa_ref[...], cba_ref[...], row)
    b = _causal_conv(_dot(xb, wb), _dot(xh, wb), cwb_ref[...], cbb_ref[...], row)
    acc_ref[...] += _dot((_silu(a) * b).astype(BF16), wd_ref[...])

    @pl.when(j == pl.num_programs(1) - 1)
    def _():
        y = _layer_norm(alpha * x_ref[...] + acc_ref[...], g_ref[...], b_ref[...])
        o_ref[...] = y
        ob_ref[...] = y.astype(ob_ref.dtype)


def _ffn(alpha, seq, xb2, x2, w_up, conv_w, conv_b, w_down, g, b):
    n, d = x2.shape
    f = w_down.shape[0]
    tm = min(TOKEN_TILE, seq)
    tf = FFN_TF
    nf = f // tf
    halo_blocks = tm // BF16_ROWS
    row = pl.BlockSpec((tm, d), lambda i, j: (i, 0))
    const2 = lambda i, j: (0, 0)
    return pl.pallas_call(
        functools.partial(_ffn_kernel, alpha, seq // tm),
        grid=(n // tm, nf),
        in_specs=[
            row,
            pl.BlockSpec((BF16_ROWS, d), lambda i, j: (jnp.maximum(i * halo_blocks - 1, 0), 0)),
            pl.BlockSpec((d, tf), lambda i, j: (0, j)),
            pl.BlockSpec((d, tf), lambda i, j: (0, nf + j)),
            pl.BlockSpec((CONV_W, tf), lambda i, j: (0, j)),
            pl.BlockSpec((CONV_W, tf), lambda i, j: (0, nf + j)),
            pl.BlockSpec((1, tf), lambda i, j: (0, j)),
            pl.BlockSpec((1, tf), lambda i, j: (0, nf + j)),
            pl.BlockSpec((tf, d), lambda i, j: (j, 0)),
            row,
            pl.BlockSpec((1, d), const2),
            pl.BlockSpec((1, d), const2),
        ],
        out_specs=[row, row],
        out_shape=[jax.ShapeDtypeStruct((n, d), F32), jax.ShapeDtypeStruct((n, d), BF16)],
        scratch_shapes=[pltpu.VMEM((tm, d), F32)],
        compiler_params=_params(("parallel", "arbitrary")),
        name="ffn",
    )(xb2, xb2, w_up, w_up, conv_w, conv_w, conv_b, conv_b, w_down, x2, g, b)


def kernel(x, w_in, hg_lower_bounds, hg_norm_g, pool_w, pool_scale, sg_ln_g, sg_ln_b, sg_w, sg_b, w_hg_proj, w_pool_proj, w_sg_proj, w_out, ln1_g, ln1_b, w_up, conv_w, conv_b, w_down, ln2_g, ln2_b):
    bsz, seq, d = x.shape
    depth = w_in.shape[0]
    n = bsz * seq
    alpha = (2 * depth) ** 0.25
    hg_width = HG_HEADS * HG_DIM
    pool_width = pool_w.shape[1] * pool_w.shape[2]
    sg_width = sg_ln_g.shape[1]
    pool_col = 4 * hg_width
    u_col = pool_col + pool_width
    v_col = u_col + sg_width
    gate_col = v_col + sg_width
    assert pool_col % pool_width == 0 and u_col % sg_width == 0 and gate_col % MERGE_TN == 0

    x2 = x.reshape(n, d)
    xb2 = x2.astype(BF16)
    lbp = hg_lower_bounds.astype(F32)
    for l in range(depth):
        w_in_l = w_in[l].astype(BF16)
        xb3 = xb2.reshape(bsz, seq, d)
        y_hg = _hgrn(l, xb3, w_in_l, lbp, hg_norm_g[l].reshape(1, hg_width)).reshape(n, hg_width)
        y_pool = _pool(xb3, w_in_l, pool_col // pool_width, pool_w[l].astype(BF16),
                       pool_scale[l].reshape(1, pool_width)).reshape(n, pool_width)
        gdim = sg_width // SG_GROUPS
        bias = jnp.repeat(sg_b[l].T, gdim, axis=1)
        y_sg = _sgu(xb2, w_in_l, u_col // sg_width, v_col // sg_width, sg_ln_g[l].reshape(1, sg_width),
                    sg_ln_b[l].reshape(1, sg_width), sg_w[l].astype(BF16), bias)
        merged = _merge(xb2, y_hg, y_pool, y_sg, w_in_l, gate_col, w_hg_proj[l].astype(BF16),
                        w_pool_proj[l].astype(BF16), w_sg_proj[l].astype(BF16))
        x2, xb2 = _outln(alpha, merged, w_out[l].astype(BF16), x2, ln1_g[l].reshape(1, d), ln1_b[l].reshape(1, d))
        x2, xb2 = _ffn(alpha, seq, xb2, x2, w_up[l].astype(BF16), conv_w[l], conv_b[l].reshape(1, -1),
                       w_down[l].astype(BF16), ln2_g[l].reshape(1, d), ln2_b[l].reshape(1, d))
    return x2.reshape(bsz, seq, d)
```

```python
import functools
import math

import jax
import jax.numpy as jnp
from jax import lax
from jax.experimental import pallas as pl
from jax.experimental.pallas import tpu as pltpu

F32 = jnp.float32
BF16 = jnp.bfloat16

HG_HEADS = 8
HG_DIM = 128
POOL_WINDOWS = (2, 4, 8, 16)
SG_GROUPS = 8
SG_CHUNK = 128
N_BRANCH = 3
CONV_W = 3
LN_EPS = 1e-5
RMS_EPS = 1e-6

LANES = 128
SUBLANES = 8
BF16_ROWS = 16
VMEM_LIMIT_BYTES = 60 * 1024 * 1024

HG_PAIR = 2 * HG_DIM
HG_CHUNK = 128
HG_LEVELS = tuple(HG_CHUNK >> (i + 1) for i in range(int(math.log2(HG_CHUNK))))
POOL_HALO = 16
TOKEN_TILE = 512
MERGE_TN = 512
FFN_TM = 1024
FFN_TF = 512
FFN_COLS = 256
FFN_ROWS = 128
LN_ROWS = 16


def _params(semantics):
    return pltpu.CompilerParams(dimension_semantics=semantics, vmem_limit_bytes=VMEM_LIMIT_BYTES)


def _dot(a, b):
    return jnp.dot(a, b, preferred_element_type=F32)


def _dot_nt(a, b):
    return lax.dot_general(a, b, (((1,), (1,)), ((), ())), preferred_element_type=F32)


def _dot_tn(a, b):
    return lax.dot_general(a, b, (((0,), (0,)), ((), ())), preferred_element_type=F32)


def _sigmoid(x):
    return 1.0 / (1.0 + jnp.exp(-x))


def _silu(x):
    return x * _sigmoid(x)


def _gelu_tanh(x):
    c = math.sqrt(2.0 / math.pi)
    return 0.5 * x * (1.0 + jnp.tanh(c * (x + 0.044715 * (x * x * x))))


def _layer_norm(y, g, b):
    mu = jnp.mean(y, axis=-1, keepdims=True)
    d = y - mu
    var = jnp.mean(d * d, axis=-1, keepdims=True)
    return d * lax.rsqrt(var + LN_EPS) * g + b


def _residual_layer_norm(alpha, x_ref, y_ref, g, b, o_ref, ob_ref):
    def body(i, carry):
        rs = pl.ds(pl.multiple_of(i * LN_ROWS, LN_ROWS), LN_ROWS)
        out = _layer_norm(alpha * x_ref[rs, :] + y_ref[rs, :], g, b)
        o_ref[rs, :] = out
        ob_ref[rs, :] = out.astype(ob_ref.dtype)
        return carry
    lax.fori_loop(0, x_ref.shape[0] // LN_ROWS, body, 0)


def _block_reference(b, h, row):
    t, c = b.shape
    if 2 * h >= SUBLANES:
        blk = b.reshape(t // (2 * h), 2 * h, c)
        ref = jnp.broadcast_to(blk[:, h - 1:h, :], blk.shape)
        return ref.reshape(t, c)
    pos = row & (2 * h - 1)
    out = b
    for p in range(2 * h):
        off = p - (h - 1)
        if off == 0:
            continue
        shifted = pltpu.roll(b, off % t, axis=0)
        out = jnp.where(pos == p, shifted, out)
    return out


def _hgrn_kernel(layer, x_ref, wq_ref, wf_ref, wi_ref, wg_ref, lbp_ref, ng_ref, o_ref, state_ref):
    @pl.when(pl.program_id(2) == 0)
    def _():
        state_ref[...] = jnp.zeros_like(state_ref)

    x = x_ref[0]
    t = x.shape[0]
    zq = _dot(x, wq_ref[...])
    zf = _dot(x, wf_ref[...])
    zi = _dot(x, wi_ref[...])
    zg = _dot(x, wg_ref[...])

    qf = _silu(zq)
    e = jnp.exp(-jnp.abs(zf))
    inv1pe = 1.0 / (1.0 + e)
    log_sig = jnp.minimum(zf, 0.0) - jnp.log(1.0 + e)
    sig_neg = jnp.where(zf >= 0.0, e * inv1pe, inv1pe)
    if layer == 0:
        lf = log_sig
        kf = sig_neg
    else:
        p = lbp_ref[...]
        rows = [p[i:i + 1, :] for i in range(p.shape[0])]
        m = functools.reduce(jnp.maximum, rows)
        ex = [jnp.exp(r - m) for r in rows]
        lb = functools.reduce(jnp.add, ex[1:layer + 1]) / functools.reduce(jnp.add, ex)
        la = jnp.log(lb)
        lbv = jnp.log(1.0 - lb) + log_sig
        lf = jnp.maximum(la, lbv) + jnp.log(1.0 + jnp.exp(-jnp.abs(la - lbv)))
        kf = (1.0 - lb) * sig_neg

    row = lax.broadcasted_iota(jnp.int32, (t, 1), 0)
    pos = row & (HG_CHUNK - 1)
    b = lf
    sh = 1
    while sh < HG_CHUNK:
        b = b + jnp.where(pos >= sh, pltpu.roll(b, sh, axis=0), 0.0)
        sh *= 2

    n_chunks = t // HG_CHUNK
    n_heads = HG_PAIR // HG_DIM
    ti = lax.broadcasted_iota(jnp.int32, (HG_CHUNK, HG_CHUNK), 0)
    si = lax.broadcasted_iota(jnp.int32, (HG_CHUNK, HG_CHUNK), 1)

    scores = [[None] * n_heads for _ in range(n_chunks)]
    for h in HG_LEVELS:
        ref = _block_reference(b, h, row)
        w = jnp.exp(-jnp.abs(b - ref))
        second = (row & h) != 0
        qt = jnp.where(second, qf * w, 0.0).astype(BF16)
        kt = jnp.where(second, 0.0, kf * w).astype(BF16)
        same = (ti // (2 * h)) == (si // (2 * h))
        for c in range(n_chunks):
            r0 = c * HG_CHUNK
            for hh in range(n_heads):
                c0 = hh * HG_DIM
                a = _dot_nt(qt[r0:r0 + HG_CHUNK, c0:c0 + HG_DIM], kt[r0:r0 + HG_CHUNK, c0:c0 + HG_DIM])
                a = jnp.where(same, a, 0.0)
                scores[c][hh] = a if scores[c][hh] is None else scores[c][hh] + a

    vb = zi.astype(BF16)
    qk = qf * kf
    qd = (qf * jnp.exp(b)).astype(BF16)
    ng = ng_ref[...]
    gate = _silu(zg)
    for hh in range(n_heads):
        c0 = hh * HG_DIM
        st = state_ref[hh]
        for c in range(n_chunks):
            r0 = c * HG_CHUNK
            rs = slice(r0, r0 + HG_CHUNK)
            cs = slice(c0, c0 + HG_DIM)
            b_c = b[rs, cs]
            b_last = b_c[HG_CHUNK - 1:HG_CHUNK, :]
            v_c = vb[rs, cs]
            o = _dot(scores[c][hh].astype(BF16), v_c)
            o = o + jnp.sum(qk[rs, cs], axis=-1, keepdims=True) * zi[rs, cs]
            o = o + _dot_nt(qd[rs, cs], st.astype(BF16))
            kd = (kf[rs, cs] * jnp.exp(b_last - b_c)).astype(BF16)
            st = jnp.exp(b_last) * st + _dot_tn(v_c, kd)
            o = o * lax.rsqrt(jnp.mean(o * o, axis=-1, keepdims=True) + RMS_EPS)
            o = o * ng[:, cs] * gate[rs, cs]
            o_ref[0, rs, cs] = o.astype(o_ref.dtype)
        state_ref[hh] = st


def _hgrn(layer, xb, w_in, lbp, ng):
    bsz, seq, d = xb.shape
    width = HG_HEADS * HG_DIM
    n_pairs = width // HG_PAIR
    ts = min(TOKEN_TILE, seq)
    wspec = lambda k: pl.BlockSpec((d, HG_PAIR), lambda b, p, s, k=k: (0, k * n_pairs + p))
    return pl.pallas_call(
        functools.partial(_hgrn_kernel, layer),
        grid=(bsz, n_pairs, seq // ts),
        in_specs=[
            pl.BlockSpec((1, ts, d), lambda b, p, s: (b, s, 0)),
            wspec(0), wspec(1), wspec(2), wspec(3),
            pl.BlockSpec((lbp.shape[0], HG_PAIR), lambda b, p, s: (0, p)),
            pl.BlockSpec((1, HG_PAIR), lambda b, p, s: (0, p)),
        ],
        out_specs=pl.BlockSpec((1, ts, HG_PAIR), lambda b, p, s: (b, s, p)),
        out_shape=jax.ShapeDtypeStruct((bsz, seq, width), BF16),
        scratch_shapes=[pltpu.VMEM((HG_PAIR // HG_DIM, HG_DIM, HG_DIM), F32)],
        compiler_params=_params(("parallel", "parallel", "arbitrary")),
        name="hgrn",
    )(xb, w_in, w_in, w_in, w_in, lbp, ng)


def _pool_kernel(x_ref, wp_ref, pw_ref, sc_ref, o_ref, halo_ref):
    s = pl.program_id(1)

    @pl.when(s == 0)
    def _():
        halo_ref[...] = jnp.zeros_like(halo_ref)

    x = x_ref[0]
    t = x.shape[0]
    p = _dot(x, wp_ref[...])
    ext = jnp.concatenate([halo_ref[...], p], axis=0)
    halo_ref[...] = p[t - POOL_HALO:, :]
    avail = s * t + lax.broadcasted_iota(jnp.int32, (t, 1), 0) + 1
    gdim = pw_ref.shape[1]
    for g, w in enumerate(POOL_WINDOWS):
        cs = slice(g * gdim, (g + 1) * gdim)
        acc = ext[:, cs]
        sh = 1
        while sh < w:
            acc = acc + pltpu.roll(acc, sh, axis=0)
            sh *= 2
        cnt = jnp.minimum(avail, w).astype(F32)
        pooled = acc[POOL_HALO:, :] * (1.0 / cnt) - p[:, cs]
        y = _dot(pooled.astype(BF16), pw_ref[g]) * sc_ref[:, cs]
        o_ref[0, :, cs] = y.astype(o_ref.dtype)


def _pool(xb, w_in, col_block, pw, sc):
    bsz, seq, d = xb.shape
    g, gdim, _ = pw.shape
    width = g * gdim
    ts = min(TOKEN_TILE, seq)
    return pl.pallas_call(
        _pool_kernel,
        grid=(bsz, seq // ts),
        in_specs=[
            pl.BlockSpec((1, ts, d), lambda b, s: (b, s, 0)),
            pl.BlockSpec((d, width), lambda b, s: (0, col_block)),
            pl.BlockSpec((g, gdim, gdim), lambda b, s: (0, 0, 0)),
            pl.BlockSpec((1, width), lambda b, s: (0, 0)),
        ],
        out_specs=pl.BlockSpec((1, ts, width), lambda b, s: (b, s, 0)),
        out_shape=jax.ShapeDtypeStruct((bsz, seq, width), BF16),
        scratch_shapes=[pltpu.VMEM((POOL_HALO, width), F32)],
        compiler_params=_params(("parallel", "arbitrary")),
        name="pool",
    )(xb, w_in, pw, sc)


def _sgu_kernel(x_ref, wu_ref, wv_ref, g_ref, b_ref, ws_ref, bias_ref, o_ref):
    x = x_ref[...]
    t = x.shape[0]
    gu = _gelu_tanh(_dot(x, wu_ref[...]))
    vn = _layer_norm(_gelu_tanh(_dot(x, wv_ref[...])), g_ref[...], b_ref[...]).astype(BF16)
    n_chunks = t // SG_CHUNK
    gdim = vn.shape[1] // SG_GROUPS
    ri = lax.broadcasted_iota(jnp.int32, (SG_CHUNK, SG_CHUNK), 0)
    ci = lax.broadcasted_iota(jnp.int32, (SG_CHUNK, SG_CHUNK), 1)
    for g in range(SG_GROUPS):
        cs = slice(g * gdim, (g + 1) * gdim)
        w = jnp.where(ri >= ci, ws_ref[g], jnp.zeros((), BF16))
        rhs = jnp.concatenate([vn[c * SG_CHUNK:(c + 1) * SG_CHUNK, cs] for c in range(n_chunks)], axis=1)
        mixed = _dot(w, rhs)
        for c in range(n_chunks):
            rs = slice(c * SG_CHUNK, (c + 1) * SG_CHUNK)
            y = gu[rs, cs] * (mixed[:, c * gdim:(c + 1) * gdim] + bias_ref[:, cs])
            o_ref[rs, cs] = y.astype(o_ref.dtype)


def _sgu(xb2, w_in, u_block, v_block, g, b, ws, bias):
    n, d = xb2.shape
    width = g.shape[1]
    ts = min(TOKEN_TILE, n)
    const2 = lambda i: (0, 0)
    return pl.pallas_call(
        _sgu_kernel,
        grid=(n // ts,),
        in_specs=[
            pl.BlockSpec((ts, d), lambda i: (i, 0)),
            pl.BlockSpec((d, width), lambda i: (0, u_block)),
            pl.BlockSpec((d, width), lambda i: (0, v_block)),
            pl.BlockSpec((1, width), const2),
            pl.BlockSpec((1, width), const2),
            pl.BlockSpec(ws.shape, lambda i: (0, 0, 0)),
            pl.BlockSpec(bias.shape, const2),
        ],
        out_specs=pl.BlockSpec((ts, width), lambda i: (i, 0)),
        out_shape=jax.ShapeDtypeStruct((n, width), BF16),
        compiler_params=_params(("parallel",)),
        name="sgu",
    )(xb2, w_in, w_in, g, b, ws, bias)


def _merge_kernel(x_ref, yh_ref, yp_ref, ys_ref, g0_ref, g1_ref, g2_ref, p0_ref, p1_ref, p2_ref, o_ref):
    x = x_ref[...]
    acc = None
    for y_ref, gw_ref, pw_ref in ((yh_ref, g0_ref, p0_ref), (yp_ref, g1_ref, p1_ref), (ys_ref, g2_ref, p2_ref)):
        term = _sigmoid(_dot(x, gw_ref[...])) * _dot(y_ref[...], pw_ref[...])
        acc = term if acc is None else acc + term
    o_ref[...] = acc.astype(o_ref.dtype)


def _merge(xb2, yh, yp, ys, w_in, gate_col0, ph, pp, ps):
    n, d = xb2.shape
    tm = min(TOKEN_TILE, n)
    tn = MERGE_TN
    gblk0 = gate_col0 // tn
    per_branch = d // tn
    yspec = lambda y: pl.BlockSpec((tm, y.shape[1]), lambda i, j: (i, 0))
    gspec = lambda k: pl.BlockSpec((d, tn), lambda i, j, k=k: (0, gblk0 + k * per_branch + j))
    pspec = lambda w: pl.BlockSpec((w.shape[0], tn), lambda i, j: (0, j))
    return pl.pallas_call(
        _merge_kernel,
        grid=(n // tm, d // tn),
        in_specs=[pl.BlockSpec((tm, d), lambda i, j: (i, 0)), yspec(yh), yspec(yp), yspec(ys),
                  gspec(0), gspec(1), gspec(2), pspec(ph), pspec(pp), pspec(ps)],
        out_specs=pl.BlockSpec((tm, tn), lambda i, j: (i, j)),
        out_shape=jax.ShapeDtypeStruct((n, d), BF16),
        compiler_params=_params(("parallel", "arbitrary")),
        name="merge",
    )(xb2, yh, yp, ys, w_in, w_in, w_in, ph, pp, ps)


def _outln_kernel(alpha, m_ref, w_ref, x_ref, g_ref, b_ref, o_ref, ob_ref, mix_ref):
    mix_ref[...] = _dot(m_ref[...], w_ref[...])
    _residual_layer_norm(alpha, x_ref, mix_ref, g_ref[...], b_ref[...], o_ref, ob_ref)


def _outln(alpha, merged, w_out, x2, g, b):
    n, d = x2.shape
    tm = min(TOKEN_TILE, n)
    const2 = lambda i: (0, 0)
    row = pl.BlockSpec((tm, d), lambda i: (i, 0))
    return pl.pallas_call(
        functools.partial(_outln_kernel, alpha),
        grid=(n // tm,),
        in_specs=[row, pl.BlockSpec((d, d), const2), row, pl.BlockSpec((1, d), const2), pl.BlockSpec((1, d), const2)],
        out_specs=[row, row],
        out_shape=[jax.ShapeDtypeStruct((n, d), F32), jax.ShapeDtypeStruct((n, d), BF16)],
        scratch_shapes=[pltpu.VMEM((tm, d), F32)],
        compiler_params=_params(("parallel",)),
        name="outln",
    )(merged, w_out, x2, g, b)


def _causal_conv(h_ref, r0, rows, c0, width, cw, cb):
    out = cb
    for k in range(CONV_W):
        tap = h_ref[pl.ds(BF16_ROWS - (CONV_W - 1) + k + r0, rows), pl.ds(c0, width)]
        out = out + cw[k:k + 1, :] * tap
    return out


def _ffn_kernel(alpha, tiles_per_seq, nf, xb_ref, xh_ref, wa_ref, wb_ref, cwa_ref, cwb_ref, cba_ref, cbb_ref,
                wd_ref, x_ref, g_ref, b_ref, o_ref, ob_ref, h_ref, gate_ref, acc_ref):
    s = pl.program_id(0)
    last = pl.num_programs(0) - 1
    ia = jnp.minimum(s, last - 1) // nf
    jb = jnp.maximum(s - 1, 0) % nf
    slot = s % 2
    tm = xb_ref.shape[0]
    tf = wa_ref.shape[1]

    @pl.when(s == 0)
    def _():
        gate_ref[...] = jnp.zeros_like(gate_ref)

    @pl.when(jb == 0)
    def _():
        acc_ref[...] = jnp.zeros_like(acc_ref)

    xb = xb_ref[...]
    xh = jnp.where(ia % tiles_per_seq == 0, jnp.zeros_like(xh_ref), xh_ref[...])
    for c0 in range(0, tf, FFN_COLS):
        for base, w_ref in ((0, wa_ref), (tf, wb_ref)):
            w = w_ref[:, pl.ds(c0, FFN_COLS)]
            h_ref[pl.ds(0, BF16_ROWS), pl.ds(base + c0, FFN_COLS)] = _dot(xh, w)
            h_ref[pl.ds(BF16_ROWS, tm), pl.ds(base + c0, FFN_COLS)] = _dot(xb, w)
        cs = pl.ds(c0, FFN_COLS)
        for r0 in range(0, tm, FFN_ROWS):
            a = _causal_conv(h_ref, r0, FFN_ROWS, c0, FFN_COLS, cwa_ref[:, cs], cba_ref[:, cs])
            b = _causal_conv(h_ref, r0, FFN_ROWS, tf + c0, FFN_COLS, cwb_ref[:, cs], cbb_ref[:, cs])
            gate_ref[slot, pl.ds(r0, FFN_ROWS), cs] = (_silu(a) * b).astype(BF16)

    acc_ref[...] += _dot(gate_ref[1 - slot], wd_ref[...])

    @pl.when(jnp.logical_and(s > 0, jb == nf - 1))
    def _():
        _residual_layer_norm(alpha, x_ref, acc_ref, g_ref[...], b_ref[...], o_ref, ob_ref)


def _ffn(alpha, seq, xb2, x2, w_up, conv_w, conv_b, w_down, g, b):
    n, d = x2.shape
    f = w_down.shape[0]
    tm = min(FFN_TM, seq)
    tf = FFN_TF
    nf = f // tf
    n_tiles = (n // tm) * nf
    halo_blocks = tm // BF16_ROWS
    up_tile = lambda s: jnp.minimum(s, n_tiles - 1)
    down_tile = lambda s: jnp.maximum(s - 1, 0)
    resid = lambda: pl.BlockSpec((tm, d), lambda s: (down_tile(s) // nf, 0), pipeline_mode=pl.Buffered(1))
    const2 = lambda s: (0, 0)
    return pl.pallas_call(
        functools.partial(_ffn_kernel, alpha, seq // tm, nf),
        grid=(n_tiles + 1,),
        in_specs=[
            pl.BlockSpec((tm, d), lambda s: (up_tile(s) // nf, 0), pipeline_mode=pl.Buffered(1)),
            pl.BlockSpec((BF16_ROWS, d), lambda s: (jnp.maximum((up_tile(s) // nf) * halo_blocks - 1, 0), 0)),
            pl.BlockSpec((d, tf), lambda s: (0, up_tile(s) % nf)),
            pl.BlockSpec((d, tf), lambda s: (0, nf + up_tile(s) % nf)),
            pl.BlockSpec((CONV_W, tf), lambda s: (0, up_tile(s) % nf)),
            pl.BlockSpec((CONV_W, tf), lambda s: (0, nf + up_tile(s) % nf)),
            pl.BlockSpec((1, tf), lambda s: (0, up_tile(s) % nf)),
            pl.BlockSpec((1, tf), lambda s: (0, nf + up_tile(s) % nf)),
            pl.BlockSpec((tf, d), lambda s: (down_tile(s) % nf, 0)),
            resid(),
            pl.BlockSpec((1, d), const2),
            pl.BlockSpec((1, d), const2),
        ],
        out_specs=[resid(), resid()],
        out_shape=[jax.ShapeDtypeStruct((n, d), F32), jax.ShapeDtypeStruct((n, d), BF16)],
        scratch_shapes=[pltpu.VMEM((BF16_ROWS + tm, 2 * tf), F32), pltpu.VMEM((2, tm, tf), BF16),
                        pltpu.VMEM((tm, d), F32)],
        compiler_params=_params(("arbitrary",)),
        name="ffn",
    )(xb2, xb2, w_up, w_up, conv_w, conv_w, conv_b, conv_b, w_down, x2, g, b)


def kernel(x, w_in, hg_lower_bounds, hg_norm_g, pool_w, pool_scale, sg_ln_g, sg_ln_b, sg_w, sg_b, w_hg_proj, w_pool_proj, w_sg_proj, w_out, ln1_g, ln1_b, w_up, conv_w, conv_b, w_down, ln2_g, ln2_b):
    bsz, seq, d = x.shape
    depth = w_in.shape[0]
    n = bsz * seq
    alpha = (2 * depth) ** 0.25
    hg_width = HG_HEADS * HG_DIM
    pool_width = pool_w.shape[1] * pool_w.shape[2]
    sg_width = sg_ln_g.shape[1]
    pool_col = 4 * hg_width
    u_col = pool_col + pool_width
    v_col = u_col + sg_width
    gate_col = v_col + sg_width
    assert pool_col % pool_width == 0 and u_col % sg_width == 0 and gate_col % MERGE_TN == 0

    x2 = x.reshape(n, d)
    xb2 = x2.astype(BF16)
    lbp = hg_lower_bounds.astype(F32)
    for l in range(depth):
        w_in_l = w_in[l].astype(BF16)
        xb3 = xb2.reshape(bsz, seq, d)
        y_hg = _hgrn(l, xb3, w_in_l, lbp, hg_norm_g[l].reshape(1, hg_width)).reshape(n, hg_width)
        y_pool = _pool(xb3, w_in_l, pool_col // pool_width, pool_w[l].astype(BF16),
                       pool_scale[l].reshape(1, pool_width)).reshape(n, pool_width)
        gdim = sg_width // SG_GROUPS
        bias = jnp.repeat(sg_b[l].T, gdim, axis=1)
        y_sg = _sgu(xb2, w_in_l, u_col // sg_width, v_col // sg_width, sg_ln_g[l].reshape(1, sg_width),
                    sg_ln_b[l].reshape(1, sg_width), sg_w[l].astype(BF16), bias)
        merged = _merge(xb2, y_hg, y_pool, y_sg, w_in_l, gate_col, w_hg_proj[l].astype(BF16),
                        w_pool_proj[l].astype(BF16), w_sg_proj[l].astype(BF16))
        x2, xb2 = _outln(alpha, merged, w_out[l].astype(BF16), x2, ln1_g[l].reshape(1, d), ln1_b[l].reshape(1, d))
        x2, xb2 = _ffn(alpha, seq, xb2, x2, w_up[l].astype(BF16), conv_w[l], conv_b[l].reshape(1, -1),
                       w_down[l].astype(BF16), ln2_g[l].reshape(1, d), ln2_b[l].reshape(1, d))
    return x2.reshape(bsz, seq, d)
```

```python
import functools
import math

import jax
import jax.numpy as jnp
from jax import lax
from jax.experimental import pallas as pl
from jax.experimental.pallas import tpu as pltpu

F32 = jnp.float32
BF16 = jnp.bfloat16

HG_HEADS = 8
HG_DIM = 128
POOL_WINDOWS = (2, 4, 8, 16)
SG_GROUPS = 8
SG_CHUNK = 128
N_BRANCH = 3
CONV_W = 3
LN_EPS = 1e-5
RMS_EPS = 1e-6

LANES = 128
SUBLANES = 8
BF16_ROWS = 16
VMEM_LIMIT_BYTES = 60 * 1024 * 1024

HG_PAIR = 2 * HG_DIM
HG_CHUNK = 128
HG_LEVELS = tuple(HG_CHUNK >> (i + 1) for i in range(int(math.log2(HG_CHUNK))))
POOL_HALO = 16
TOKEN_TILE = 512
MERGE_TN = 512
FFN_TM = 1024
FFN_TF = 512
FFN_COLS = 256
FFN_ROWS = 128
LN_ROWS = 128


def _params(semantics):
    return pltpu.CompilerParams(dimension_semantics=semantics, vmem_limit_bytes=VMEM_LIMIT_BYTES)


def _dot(a, b):
    return jnp.dot(a, b, preferred_element_type=F32)


def _dot_nt(a, b):
    return lax.dot_general(a, b, (((1,), (1,)), ((), ())), preferred_element_type=F32)


def _dot_tn(a, b):
    return lax.dot_general(a, b, (((0,), (0,)), ((), ())), preferred_element_type=F32)


def _sigmoid(x):
    return 1.0 / (1.0 + jnp.exp(-x))


def _silu(x):
    return x * _sigmoid(x)


def _gelu_tanh(x):
    c = math.sqrt(2.0 / math.pi)
    return 0.5 * x * (1.0 + jnp.tanh(c * (x + 0.044715 * (x * x * x))))


def _layer_norm(y, g, b):
    mu = jnp.mean(y, axis=-1, keepdims=True)
    d = y - mu
    var = jnp.mean(d * d, axis=-1, keepdims=True)
    return d * lax.rsqrt(var + LN_EPS) * g + b


def _residual_layer_norm(alpha, x_ref, y_ref, g, b, o_ref, ob_ref):
    def body(i, carry):
        rs = pl.ds(pl.multiple_of(i * LN_ROWS, LN_ROWS), LN_ROWS)
        out = _layer_norm(alpha * x_ref[rs, :] + y_ref[rs, :], g, b)
        o_ref[rs, :] = out
        ob_ref[rs, :] = out.astype(ob_ref.dtype)
        return carry
    lax.fori_loop(0, x_ref.shape[0] // LN_ROWS, body, 0)


def _block_reference(b, h, row):
    t, c = b.shape
    if 2 * h >= SUBLANES:
        blk = b.reshape(t // (2 * h), 2 * h, c)
        ref = jnp.broadcast_to(blk[:, h - 1:h, :], blk.shape)
        return ref.reshape(t, c)
    pos = row & (2 * h - 1)
    out = b
    for p in range(2 * h):
        off = p - (h - 1)
        if off == 0:
            continue
        shifted = pltpu.roll(b, off % t, axis=0)
        out = jnp.where(pos == p, shifted, out)
    return out


def _hgrn_kernel(layer, x_ref, wq_ref, wf_ref, wi_ref, wg_ref, lbp_ref, ng_ref, o_ref, w_ref, state_ref):
    @pl.when(pl.program_id(2) == 0)
    def _():
        state_ref[...] = jnp.zeros_like(state_ref)
        for k, src in enumerate((wq_ref, wf_ref, wi_ref, wg_ref)):
            w_ref[k] = src[...].astype(BF16)

    x = x_ref[0]
    t = x.shape[0]
    zq = _dot(x, w_ref[0])
    zf = _dot(x, w_ref[1])
    zi = _dot(x, w_ref[2])
    zg = _dot(x, w_ref[3])

    qf = _silu(zq)
    e = jnp.exp(-jnp.abs(zf))
    inv1pe = 1.0 / (1.0 + e)
    log_sig = jnp.minimum(zf, 0.0) - jnp.log(1.0 + e)
    sig_neg = jnp.where(zf >= 0.0, e * inv1pe, inv1pe)
    if layer == 0:
        lf = log_sig
        kf = sig_neg
    else:
        p = lbp_ref[...]
        rows = [p[i:i + 1, :] for i in range(p.shape[0])]
        m = functools.reduce(jnp.maximum, rows)
        ex = [jnp.exp(r - m) for r in rows]
        lb = functools.reduce(jnp.add, ex[1:layer + 1]) / functools.reduce(jnp.add, ex)
        la = jnp.log(lb)
        lbv = jnp.log(1.0 - lb) + log_sig
        lf = jnp.maximum(la, lbv) + jnp.log(1.0 + jnp.exp(-jnp.abs(la - lbv)))
        kf = (1.0 - lb) * sig_neg

    row = lax.broadcasted_iota(jnp.int32, (t, 1), 0)
    pos = row & (HG_CHUNK - 1)
    b = lf
    sh = 1
    while sh < HG_CHUNK:
        b = b + jnp.where(pos >= sh, pltpu.roll(b, sh, axis=0), 0.0)
        sh *= 2

    n_chunks = t // HG_CHUNK
    n_heads = HG_PAIR // HG_DIM
    ti = lax.broadcasted_iota(jnp.int32, (HG_CHUNK, HG_CHUNK), 0)
    si = lax.broadcasted_iota(jnp.int32, (HG_CHUNK, HG_CHUNK), 1)

    scores = [[None] * n_heads for _ in range(n_chunks)]
    for h in HG_LEVELS:
        ref = _block_reference(b, h, row)
        w = jnp.exp(-jnp.abs(b - ref))
        second = (row & h) != 0
        qt = jnp.where(second, qf * w, 0.0).astype(BF16)
        kt = jnp.where(second, 0.0, kf * w).astype(BF16)
        same = (ti // (2 * h)) == (si // (2 * h))
        for c in range(n_chunks):
            r0 = c * HG_CHUNK
            for hh in range(n_heads):
                c0 = hh * HG_DIM
                a = _dot_nt(qt[r0:r0 + HG_CHUNK, c0:c0 + HG_DIM], kt[r0:r0 + HG_CHUNK, c0:c0 + HG_DIM])
                a = jnp.where(same, a, 0.0)
                scores[c][hh] = a if scores[c][hh] is None else scores[c][hh] + a

    vb = zi.astype(BF16)
    qk = qf * kf
    qd = (qf * jnp.exp(b)).astype(BF16)
    ng = ng_ref[...]
    gate = _silu(zg)
    for hh in range(n_heads):
        c0 = hh * HG_DIM
        st = state_ref[hh]
        for c in range(n_chunks):
            r0 = c * HG_CHUNK
            rs = slice(r0, r0 + HG_CHUNK)
            cs = slice(c0, c0 + HG_DIM)
            b_c = b[rs, cs]
            b_last = b_c[HG_CHUNK - 1:HG_CHUNK, :]
            v_c = vb[rs, cs]
            o = _dot(scores[c][hh].astype(BF16), v_c)
            o = o + jnp.sum(qk[rs, cs], axis=-1, keepdims=True) * zi[rs, cs]
            o = o + _dot_nt(qd[rs, cs], st.astype(BF16))
            kd = (kf[rs, cs] * jnp.exp(b_last - b_c)).astype(BF16)
            st = jnp.exp(b_last) * st + _dot_tn(v_c, kd)
            o = o * lax.rsqrt(jnp.mean(o * o, axis=-1, keepdims=True) + RMS_EPS)
            o = o * ng[:, cs] * gate[rs, cs]
            o_ref[0, rs, cs] = o.astype(o_ref.dtype)
        state_ref[hh] = st


def _hgrn(layer, xb, w_in, lbp, ng):
    bsz, seq, d = xb.shape
    width = HG_HEADS * HG_DIM
    n_pairs = width // HG_PAIR
    ts = min(TOKEN_TILE, seq)
    wspec = lambda k: pl.BlockSpec((None, d, HG_PAIR), lambda b, p, s, k=k: (layer, 0, k * n_pairs + p))
    return pl.pallas_call(
        functools.partial(_hgrn_kernel, layer),
        grid=(bsz, n_pairs, seq // ts),
        in_specs=[
            pl.BlockSpec((1, ts, d), lambda b, p, s: (b, s, 0)),
            wspec(0), wspec(1), wspec(2), wspec(3),
            pl.BlockSpec((lbp.shape[0], HG_PAIR), lambda b, p, s: (0, p)),
            pl.BlockSpec((1, HG_PAIR), lambda b, p, s: (0, p)),
        ],
        out_specs=pl.BlockSpec((1, ts, HG_PAIR), lambda b, p, s: (b, s, p)),
        out_shape=jax.ShapeDtypeStruct((bsz, seq, width), BF16),
        scratch_shapes=[pltpu.VMEM((4, d, HG_PAIR), BF16), pltpu.VMEM((HG_PAIR // HG_DIM, HG_DIM, HG_DIM), F32)],
        compiler_params=_params(("parallel", "parallel", "arbitrary")),
        name="hgrn",
    )(xb, w_in, w_in, w_in, w_in, lbp, ng)


def _pool_kernel(x_ref, wp_ref, pw_ref, sc_ref, o_ref, w_ref, halo_ref):
    s = pl.program_id(1)

    @pl.when(jnp.logical_and(pl.program_id(0) == 0, s == 0))
    def _():
        w_ref[...] = wp_ref[...].astype(BF16)

    @pl.when(s == 0)
    def _():
        halo_ref[...] = jnp.zeros_like(halo_ref)

    x = x_ref[0]
    t = x.shape[0]
    p = _dot(x, w_ref[...])
    ext = jnp.concatenate([halo_ref[...], p], axis=0)
    halo_ref[...] = p[t - POOL_HALO:, :]
    avail = s * t + lax.broadcasted_iota(jnp.int32, (t, 1), 0) + 1
    gdim = pw_ref.shape[1]
    for g, w in enumerate(POOL_WINDOWS):
        cs = slice(g * gdim, (g + 1) * gdim)
        acc = ext[:, cs]
        sh = 1
        while sh < w:
            acc = acc + pltpu.roll(acc, sh, axis=0)
            sh *= 2
        cnt = jnp.minimum(avail, w).astype(F32)
        pooled = acc[POOL_HALO:, :] * (1.0 / cnt) - p[:, cs]
        y = _dot(pooled.astype(BF16), pw_ref[g].astype(BF16)) * sc_ref[:, cs]
        o_ref[0, :, cs] = y.astype(o_ref.dtype)


def _pool(layer, xb, w_in, col_block, pw, sc):
    bsz, seq, d = xb.shape
    _, g, gdim, _ = pw.shape
    width = g * gdim
    ts = min(TOKEN_TILE, seq)
    return pl.pallas_call(
        _pool_kernel,
        grid=(bsz, seq // ts),
        in_specs=[
            pl.BlockSpec((1, ts, d), lambda b, s: (b, s, 0)),
            pl.BlockSpec((None, d, width), lambda b, s: (layer, 0, col_block), pipeline_mode=pl.Buffered(1)),
            pl.BlockSpec((None, g, gdim, gdim), lambda b, s: (layer, 0, 0, 0)),
            pl.BlockSpec((1, width), lambda b, s: (0, 0)),
        ],
        out_specs=pl.BlockSpec((1, ts, width), lambda b, s: (b, s, 0)),
        out_shape=jax.ShapeDtypeStruct((bsz, seq, width), BF16),
        scratch_shapes=[pltpu.VMEM((d, width), BF16), pltpu.VMEM((POOL_HALO, width), F32)],
        compiler_params=_params(("arbitrary", "arbitrary")),
        name="pool",
    )(xb, w_in, pw, sc)


def _sgu_kernel(x_ref, wu_ref, wv_ref, g_ref, b_ref, ws_ref, bias_ref, o_ref, w_ref):
    @pl.when(pl.program_id(0) == 0)
    def _():
        w_ref[0] = wu_ref[...].astype(BF16)
        w_ref[1] = wv_ref[...].astype(BF16)

    x = x_ref[...]
    t = x.shape[0]
    gu = _gelu_tanh(_dot(x, w_ref[0]))
    vn = _layer_norm(_gelu_tanh(_dot(x, w_ref[1])), g_ref[...], b_ref[...]).astype(BF16)
    n_chunks = t // SG_CHUNK
    gdim = vn.shape[1] // SG_GROUPS
    ri = lax.broadcasted_iota(jnp.int32, (SG_CHUNK, SG_CHUNK), 0)
    ci = lax.broadcasted_iota(jnp.int32, (SG_CHUNK, SG_CHUNK), 1)
    for g in range(SG_GROUPS):
        cs = slice(g * gdim, (g + 1) * gdim)
        w = jnp.where(ri >= ci, ws_ref[g], 0.0).astype(BF16)
        rhs = jnp.concatenate([vn[c * SG_CHUNK:(c + 1) * SG_CHUNK, cs] for c in range(n_chunks)], axis=1)
        mixed = _dot(w, rhs)
        for c in range(n_chunks):
            rs = slice(c * SG_CHUNK, (c + 1) * SG_CHUNK)
            y = gu[rs, cs] * (mixed[:, c * gdim:(c + 1) * gdim] + bias_ref[:, cs])
            o_ref[rs, cs] = y.astype(o_ref.dtype)


def _sgu(layer, xb2, w_in, u_block, v_block, g, b, ws, bias):
    n, d = xb2.shape
    width = g.shape[1]
    ts = min(TOKEN_TILE, n)
    const2 = lambda i: (0, 0)
    return pl.pallas_call(
        _sgu_kernel,
        grid=(n // ts,),
        in_specs=[
            pl.BlockSpec((ts, d), lambda i: (i, 0)),
            pl.BlockSpec((None, d, width), lambda i: (layer, 0, u_block), pipeline_mode=pl.Buffered(1)),
            pl.BlockSpec((None, d, width), lambda i: (layer, 0, v_block), pipeline_mode=pl.Buffered(1)),
            pl.BlockSpec((1, width), const2),
            pl.BlockSpec((1, width), const2),
            pl.BlockSpec((None,) + ws.shape[1:], lambda i: (layer, 0, 0, 0)),
            pl.BlockSpec(bias.shape, const2),
        ],
        out_specs=pl.BlockSpec((ts, width), lambda i: (i, 0)),
        out_shape=jax.ShapeDtypeStruct((n, width), BF16),
        scratch_shapes=[pltpu.VMEM((2, d, width), BF16)],
        compiler_params=_params(("arbitrary",)),
        name="sgu",
    )(xb2, w_in, w_in, g, b, ws, bias)


def _merge_kernel(x_ref, yh_ref, yp_ref, ys_ref, g0_ref, g1_ref, g2_ref, p0_ref, p1_ref, p2_ref, o_ref,
                  gw_ref, pw_ref):
    @pl.when(pl.program_id(1) == 0)
    def _():
        for k, (g_ref, p_ref) in enumerate(((g0_ref, p0_ref), (g1_ref, p1_ref), (g2_ref, p2_ref))):
            gw_ref[k] = g_ref[...].astype(BF16)
            pw_ref[k] = p_ref[...].astype(BF16)

    x = x_ref[...]
    acc = None
    for k, y_ref in enumerate((yh_ref, yp_ref, ys_ref)):
        term = _sigmoid(_dot(x, gw_ref[k])) * _dot(y_ref[...], pw_ref[k])
        acc = term if acc is None else acc + term
    o_ref[...] = acc.astype(o_ref.dtype)


def _merge(layer, xb2, yh, yp, ys, w_in, gate_col0, ph, pp, ps):
    n, d = xb2.shape
    tm = min(TOKEN_TILE, n)
    tn = MERGE_TN
    gblk0 = gate_col0 // tn
    per_branch = d // tn
    width = ph.shape[1]
    assert pp.shape[1] == width and ps.shape[1] == width
    once = pl.Buffered(1)
    yspec = lambda y: pl.BlockSpec((tm, y.shape[1]), lambda j, i: (i, 0))
    gspec = lambda k: pl.BlockSpec((None, d, tn), lambda j, i, k=k: (layer, 0, gblk0 + k * per_branch + j),
                                   pipeline_mode=once)
    pspec = lambda: pl.BlockSpec((None, width, tn), lambda j, i: (layer, 0, j), pipeline_mode=once)
    return pl.pallas_call(
        _merge_kernel,
        grid=(d // tn, n // tm),
        in_specs=[pl.BlockSpec((tm, d), lambda j, i: (i, 0)), yspec(yh), yspec(yp), yspec(ys),
                  gspec(0), gspec(1), gspec(2), pspec(), pspec(), pspec()],
        out_specs=pl.BlockSpec((tm, tn), lambda j, i: (i, j)),
        out_shape=jax.ShapeDtypeStruct((n, d), BF16),
        scratch_shapes=[pltpu.VMEM((N_BRANCH, d, tn), BF16), pltpu.VMEM((N_BRANCH, width, tn), BF16)],
        compiler_params=_params(("arbitrary", "arbitrary")),
        name="merge",
    )(xb2, yh, yp, ys, w_in, w_in, w_in, ph, pp, ps)


def _outln_kernel(alpha, m_ref, w_ref, x_ref, g_ref, b_ref, o_ref, ob_ref, wb_ref, mix_ref):
    @pl.when(pl.program_id(0) == 0)
    def _():
        wb_ref[...] = w_ref[...].astype(BF16)

    mix_ref[...] = _dot(m_ref[...], wb_ref[...])
    _residual_layer_norm(alpha, x_ref, mix_ref, g_ref[...], b_ref[...], o_ref, ob_ref)


def _outln(alpha, layer, merged, w_out, x2, g, b):
    n, d = x2.shape
    tm = min(TOKEN_TILE // 2, n)
    const2 = lambda i: (0, 0)
    row = pl.BlockSpec((tm, d), lambda i: (i, 0))
    return pl.pallas_call(
        functools.partial(_outln_kernel, alpha),
        grid=(n // tm,),
        in_specs=[row, pl.BlockSpec((None, d, d), lambda i: (layer, 0, 0), pipeline_mode=pl.Buffered(1)), row,
                  pl.BlockSpec((1, d), const2), pl.BlockSpec((1, d), const2)],
        out_specs=[row, row],
        out_shape=[jax.ShapeDtypeStruct((n, d), F32), jax.ShapeDtypeStruct((n, d), BF16)],
        scratch_shapes=[pltpu.VMEM((d, d), BF16), pltpu.VMEM((tm, d), F32)],
        compiler_params=_params(("arbitrary",)),
        name="outln",
    )(merged, w_out, x2, g, b)


def _causal_conv(h_ref, r0, rows, c0, width, cw, cb):
    out = cb
    for k in range(CONV_W):
        tap = h_ref[pl.ds(BF16_ROWS - (CONV_W - 1) + k + r0, rows), pl.ds(c0, width)]
        out = out + cw[k:k + 1, :] * tap
    return out


def _ffn_kernel(alpha, tiles_per_seq, nf, xb_ref, xh_ref, wa_ref, wb_ref, cwa_ref, cwb_ref, cba_ref, cbb_ref,
                wd_ref, x_ref, g_ref, b_ref, o_ref, ob_ref, h_ref, gate_ref, acc_ref):
    s = pl.program_id(0)
    last = pl.num_programs(0) - 1
    ia = jnp.minimum(s, last - 1) // nf
    jb = jnp.maximum(s - 1, 0) % nf
    slot = s % 2
    tm = xb_ref.shape[0]
    tf = wa_ref.shape[1]

    @pl.when(s == 0)
    def _():
        gate_ref[...] = jnp.zeros_like(gate_ref)

    @pl.when(jb == 0)
    def _():
        acc_ref[...] = jnp.zeros_like(acc_ref)

    xb = xb_ref[...]
    xh = jnp.where(ia % tiles_per_seq == 0, jnp.zeros_like(xh_ref), xh_ref[...])
    for c0 in range(0, tf, FFN_COLS):
        for base, w_ref in ((0, wa_ref), (tf, wb_ref)):
            w = w_ref[:, pl.ds(c0, FFN_COLS)]
            h_ref[pl.ds(0, BF16_ROWS), pl.ds(base + c0, FFN_COLS)] = _dot(xh, w)
            h_ref[pl.ds(BF16_ROWS, tm), pl.ds(base + c0, FFN_COLS)] = _dot(xb, w)
        cs = pl.ds(c0, FFN_COLS)
        for r0 in range(0, tm, FFN_ROWS):
            a = _causal_conv(h_ref, r0, FFN_ROWS, c0, FFN_COLS, cwa_ref[:, cs], cba_ref[:, cs])
            b = _causal_conv(h_ref, r0, FFN_ROWS, tf + c0, FFN_COLS, cwb_ref[:, cs], cbb_ref[:, cs])
            gate_ref[slot, pl.ds(r0, FFN_ROWS), cs] = (_silu(a) * b).astype(BF16)

    acc_ref[...] += _dot(gate_ref[1 - slot], wd_ref[...])

    @pl.when(jnp.logical_and(s > 0, jb == nf - 1))
    def _():
        _residual_layer_norm(alpha, x_ref, acc_ref, g_ref[...], b_ref[...], o_ref, ob_ref)


def _ffn(alpha, seq, xb2, x2, w_up, conv_w, conv_b, w_down, g, b):
    n, d = x2.shape
    f = w_down.shape[0]
    tm = min(FFN_TM, seq)
    tf = FFN_TF
    nf = f // tf
    n_tiles = (n // tm) * nf
    halo_blocks = tm // BF16_ROWS
    up_tile = lambda s: jnp.minimum(s, n_tiles - 1)
    down_tile = lambda s: jnp.maximum(s - 1, 0)
    resid = lambda: pl.BlockSpec((tm, d), lambda s: (down_tile(s) // nf, 0), pipeline_mode=pl.Buffered(1))
    const2 = lambda s: (0, 0)
    return pl.pallas_call(
        functools.partial(_ffn_kernel, alpha, seq // tm, nf),
        grid=(n_tiles + 1,),
        in_specs=[
            pl.BlockSpec((tm, d), lambda s: (up_tile(s) // nf, 0), pipeline_mode=pl.Buffered(1)),
            pl.BlockSpec((BF16_ROWS, d), lambda s: (jnp.maximum((up_tile(s) // nf) * halo_blocks - 1, 0), 0)),
            pl.BlockSpec((d, tf), lambda s: (0, up_tile(s) % nf)),
            pl.BlockSpec((d, tf), lambda s: (0, nf + up_tile(s) % nf)),
            pl.BlockSpec((CONV_W, tf), lambda s: (0, up_tile(s) % nf)),
            pl.BlockSpec((CONV_W, tf), lambda s: (0, nf + up_tile(s) % nf)),
            pl.BlockSpec((1, tf), lambda s: (0, up_tile(s) % nf)),
            pl.BlockSpec((1, tf), lambda s: (0, nf + up_tile(s) % nf)),
            pl.BlockSpec((tf, d), lambda s: (down_tile(s) % nf, 0)),
            resid(),
            pl.BlockSpec((1, d), const2),
            pl.BlockSpec((1, d), const2),
        ],
        out_specs=[resid(), resid()],
        out_shape=[jax.ShapeDtypeStruct((n, d), F32), jax.ShapeDtypeStruct((n, d), BF16)],
        scratch_shapes=[pltpu.VMEM((BF16_ROWS + tm, 2 * tf), F32), pltpu.VMEM((2, tm, tf), BF16),
                        pltpu.VMEM((tm, d), F32)],
        compiler_params=_params(("arbitrary",)),
        name="ffn",
    )(xb2, xb2, w_up, w_up, conv_w, conv_w, conv_b, conv_b, w_down, x2, g, b)


def kernel(x, w_in, hg_lower_bounds, hg_norm_g, pool_w, pool_scale, sg_ln_g, sg_ln_b, sg_w, sg_b, w_hg_proj, w_pool_proj, w_sg_proj, w_out, ln1_g, ln1_b, w_up, conv_w, conv_b, w_down, ln2_g, ln2_b):
    bsz, seq, d = x.shape
    depth = w_in.shape[0]
    n = bsz * seq
    alpha = (2 * depth) ** 0.25
    hg_width = HG_HEADS * HG_DIM
    pool_width = pool_w.shape[1] * pool_w.shape[2]
    sg_width = sg_ln_g.shape[1]
    pool_col = 4 * hg_width
    u_col = pool_col + pool_width
    v_col = u_col + sg_width
    gate_col = v_col + sg_width
    assert pool_col % pool_width == 0 and u_col % sg_width == 0 and gate_col % MERGE_TN == 0

    x2 = x.reshape(n, d)
    xb2 = x2.astype(BF16)
    lbp = hg_lower_bounds.astype(F32)
    for l in range(depth):
        xb3 = xb2.reshape(bsz, seq, d)
        y_hg = _hgrn(l, xb3, w_in, lbp, hg_norm_g[l].reshape(1, hg_width)).reshape(n, hg_width)
        y_pool = _pool(l, xb3, w_in, pool_col // pool_width, pool_w,
                       pool_scale[l].reshape(1, pool_width)).reshape(n, pool_width)
        gdim = sg_width // SG_GROUPS
        bias = jnp.repeat(sg_b[l].T, gdim, axis=1)
        y_sg = _sgu(l, xb2, w_in, u_col // sg_width, v_col // sg_width, sg_ln_g[l].reshape(1, sg_width),
                    sg_ln_b[l].reshape(1, sg_width), sg_w, bias)
        merged = _merge(l, xb2, y_hg, y_pool, y_sg, w_in, gate_col, w_hg_proj, w_pool_proj, w_sg_proj)
        x2, xb2 = _outln(alpha, l, merged, w_out, x2, ln1_g[l].reshape(1, d), ln1_b[l].reshape(1, d))
        x2, xb2 = _ffn(alpha, seq, xb2, x2, w_up[l].astype(BF16), conv_w[l], conv_b[l].reshape(1, -1),
                       w_down[l].astype(BF16), ln2_g[l].reshape(1, d), ln2_b[l].reshape(1, d))
    return x2.reshape(bsz, seq, d)
```

```python
import functools
import math

import jax
import jax.numpy as jnp
from jax import lax
from jax.experimental import pallas as pl
from jax.experimental.pallas import tpu as pltpu

F32 = jnp.float32
BF16 = jnp.bfloat16

HG_HEADS = 8
HG_DIM = 128
POOL_WINDOWS = (2, 4, 8, 16)
SG_GROUPS = 8
SG_CHUNK = 128
N_BRANCH = 3
CONV_W = 3
LN_EPS = 1e-5
RMS_EPS = 1e-6

LANES = 128
SUBLANES = 8
BF16_ROWS = 16
VMEM_LIMIT_BYTES = 60 * 1024 * 1024

HG_PAIR = 2 * HG_DIM
HG_CHUNK = 128
HG_LEVELS = tuple(HG_CHUNK >> (i + 1) for i in range(int(math.log2(HG_CHUNK))))
POOL_HALO = 16
TOKEN_TILE = 512
MERGE_TN = 256
FFN_TM = 512
FFN_TF = 512
FFN_COLS = 256
FFN_ROWS = 128
LN_ROWS = 128


def _params(semantics):
    return pltpu.CompilerParams(dimension_semantics=semantics, vmem_limit_bytes=VMEM_LIMIT_BYTES)


def _dot(a, b):
    return jnp.dot(a, b, preferred_element_type=F32)


def _dot_nt(a, b):
    return lax.dot_general(a, b, (((1,), (1,)), ((), ())), preferred_element_type=F32)


def _dot_tn(a, b):
    return lax.dot_general(a, b, (((0,), (0,)), ((), ())), preferred_element_type=F32)


def _sigmoid(x):
    return 1.0 / (1.0 + jnp.exp(-x))


def _silu(x):
    return x * _sigmoid(x)


def _gelu_tanh(x):
    c = math.sqrt(2.0 / math.pi)
    return 0.5 * x * (1.0 + jnp.tanh(c * (x + 0.044715 * (x * x * x))))


def _layer_norm(y, g, b):
    mu = jnp.mean(y, axis=-1, keepdims=True)
    d = y - mu
    var = jnp.mean(d * d, axis=-1, keepdims=True)
    return d * lax.rsqrt(var + LN_EPS) * g + b


def _residual_layer_norm(alpha, x_ref, y_ref, g, b, o_ref, ob_ref):
    def body(i, carry):
        rs = pl.ds(pl.multiple_of(i * LN_ROWS, LN_ROWS), LN_ROWS)
        out = _layer_norm(alpha * x_ref[rs, :] + y_ref[rs, :], g, b)
        o_ref[rs, :] = out
        ob_ref[rs, :] = out.astype(ob_ref.dtype)
        return carry
    lax.fori_loop(0, x_ref.shape[0] // LN_ROWS, body, 0)


def _block_reference(b, h, row):
    t, c = b.shape
    if 2 * h >= SUBLANES:
        blk = b.reshape(t // (2 * h), 2 * h, c)
        ref = jnp.broadcast_to(blk[:, h - 1:h, :], blk.shape)
        return ref.reshape(t, c)
    pos = row & (2 * h - 1)
    out = b
    for p in range(2 * h):
        off = p - (h - 1)
        if off == 0:
            continue
        shifted = pltpu.roll(b, off % t, axis=0)
        out = jnp.where(pos == p, shifted, out)
    return out


def _hgrn_kernel(layer, x_ref, wq_ref, wf_ref, wi_ref, wg_ref, lbp_ref, ng_ref, up_ref, down_ref,
                 o_ref, upb_ref, downb_ref, w_ref, state_ref):
    upb_ref[...] = up_ref[...].astype(BF16)
    downb_ref[...] = down_ref[...].astype(BF16)

    @pl.when(pl.program_id(2) == 0)
    def _():
        state_ref[...] = jnp.zeros_like(state_ref)
        for k, src in enumerate((wq_ref, wf_ref, wi_ref, wg_ref)):
            w_ref[k] = src[...].astype(BF16)

    x = x_ref[0]
    t = x.shape[0]
    zq = _dot(x, w_ref[0])
    zf = _dot(x, w_ref[1])
    zi = _dot(x, w_ref[2])
    zg = _dot(x, w_ref[3])

    qf = _silu(zq)
    e = jnp.exp(-jnp.abs(zf))
    inv1pe = 1.0 / (1.0 + e)
    log_sig = jnp.minimum(zf, 0.0) - jnp.log(1.0 + e)
    sig_neg = jnp.where(zf >= 0.0, e * inv1pe, inv1pe)
    if layer == 0:
        lf = log_sig
        kf = sig_neg
    else:
        p = lbp_ref[...]
        rows = [p[i:i + 1, :] for i in range(p.shape[0])]
        m = functools.reduce(jnp.maximum, rows)
        ex = [jnp.exp(r - m) for r in rows]
        lb = functools.reduce(jnp.add, ex[1:layer + 1]) / functools.reduce(jnp.add, ex)
        la = jnp.log(lb)
        lbv = jnp.log(1.0 - lb) + log_sig
        lf = jnp.maximum(la, lbv) + jnp.log(1.0 + jnp.exp(-jnp.abs(la - lbv)))
        kf = (1.0 - lb) * sig_neg

    row = lax.broadcasted_iota(jnp.int32, (t, 1), 0)
    pos = row & (HG_CHUNK - 1)
    b = lf
    sh = 1
    while sh < HG_CHUNK:
        b = b + jnp.where(pos >= sh, pltpu.roll(b, sh, axis=0), 0.0)
        sh *= 2

    n_chunks = t // HG_CHUNK
    n_heads = HG_PAIR // HG_DIM
    ti = lax.broadcasted_iota(jnp.int32, (HG_CHUNK, HG_CHUNK), 0)
    si = lax.broadcasted_iota(jnp.int32, (HG_CHUNK, HG_CHUNK), 1)

    scores = [[None] * n_heads for _ in range(n_chunks)]
    for h in HG_LEVELS:
        ref = _block_reference(b, h, row)
        w = jnp.exp(-jnp.abs(b - ref))
        second = (row & h) != 0
        qt = jnp.where(second, qf * w, 0.0).astype(BF16)
        kt = jnp.where(second, 0.0, kf * w).astype(BF16)
        same = (ti // (2 * h)) == (si // (2 * h))
        for c in range(n_chunks):
            r0 = c * HG_CHUNK
            for hh in range(n_heads):
                c0 = hh * HG_DIM
                a = _dot_nt(qt[r0:r0 + HG_CHUNK, c0:c0 + HG_DIM], kt[r0:r0 + HG_CHUNK, c0:c0 + HG_DIM])
                a = jnp.where(same, a, 0.0)
                scores[c][hh] = a if scores[c][hh] is None else scores[c][hh] + a

    vb = zi.astype(BF16)
    qk = qf * kf
    qd = (qf * jnp.exp(b)).astype(BF16)
    ng = ng_ref[...]
    gate = _silu(zg)
    for hh in range(n_heads):
        c0 = hh * HG_DIM
        st = state_ref[hh]
        for c in range(n_chunks):
            r0 = c * HG_CHUNK
            rs = slice(r0, r0 + HG_CHUNK)
            cs = slice(c0, c0 + HG_DIM)
            b_c = b[rs, cs]
            b_last = b_c[HG_CHUNK - 1:HG_CHUNK, :]
            v_c = vb[rs, cs]
            o = _dot(scores[c][hh].astype(BF16), v_c)
            o = o + jnp.sum(qk[rs, cs], axis=-1, keepdims=True) * zi[rs, cs]
            o = o + _dot_nt(qd[rs, cs], st.astype(BF16))
            kd = (kf[rs, cs] * jnp.exp(b_last - b_c)).astype(BF16)
            st = jnp.exp(b_last) * st + _dot_tn(v_c, kd)
            o = o * lax.rsqrt(jnp.mean(o * o, axis=-1, keepdims=True) + RMS_EPS)
            o = o * ng[:, cs] * gate[rs, cs]
            o_ref[0, rs, cs] = o.astype(o_ref.dtype)
        state_ref[hh] = st


def _hgrn(layer, xb, w_in, lbp, ng, w_up, w_down):
    bsz, seq, d = xb.shape
    width = HG_HEADS * HG_DIM
    n_pairs = width // HG_PAIR
    ts = min(TOKEN_TILE, seq)
    n_s = seq // ts
    n_steps = bsz * n_pairs * n_s
    depth = w_up.shape[0]
    slab_rows = n_steps * BF16_ROWS

    def slabs(w):
        cols = (w.size // depth) // slab_rows
        assert cols * slab_rows * depth == w.size and cols % LANES == 0
        return w.reshape(depth, slab_rows, cols)

    up_slabs, down_slabs = slabs(w_up), slabs(w_down)
    step = lambda b, p, s: (b * n_pairs + p) * n_s + s
    slab_in = lambda w: pl.BlockSpec((None, BF16_ROWS, w.shape[2]), lambda b, p, s: (layer, step(b, p, s), 0))
    slab_out = lambda w: pl.BlockSpec((BF16_ROWS, w.shape[2]), lambda b, p, s: (step(b, p, s), 0))
    wspec = lambda k: pl.BlockSpec((None, d, HG_PAIR), lambda b, p, s, k=k: (layer, 0, k * n_pairs + p))
    y, upb, downb = pl.pallas_call(
        functools.partial(_hgrn_kernel, layer),
        grid=(bsz, n_pairs, n_s),
        in_specs=[
            pl.BlockSpec((1, ts, d), lambda b, p, s: (b, s, 0)),
            wspec(0), wspec(1), wspec(2), wspec(3),
            pl.BlockSpec((lbp.shape[0], HG_PAIR), lambda b, p, s: (0, p)),
            pl.BlockSpec((1, HG_PAIR), lambda b, p, s: (0, p)),
            slab_in(up_slabs), slab_in(down_slabs),
        ],
        out_specs=[pl.BlockSpec((1, ts, HG_PAIR), lambda b, p, s: (b, s, p)), slab_out(up_slabs), slab_out(down_slabs)],
        out_shape=[jax.ShapeDtypeStruct((bsz, seq, width), BF16),
                   jax.ShapeDtypeStruct(up_slabs.shape[1:], BF16), jax.ShapeDtypeStruct(down_slabs.shape[1:], BF16)],
        scratch_shapes=[pltpu.VMEM((4, d, HG_PAIR), BF16), pltpu.VMEM((HG_PAIR // HG_DIM, HG_DIM, HG_DIM), F32)],
        compiler_params=_params(("arbitrary", "arbitrary", "arbitrary")),
        name="hgrn",
    )(xb, w_in, w_in, w_in, w_in, lbp, ng, up_slabs, down_slabs)
    return y, upb.reshape(w_up.shape[1:]), downb.reshape(w_down.shape[1:])


def _pool_kernel(x_ref, wp_ref, pw_ref, sc_ref, o_ref, w_ref, halo_ref):
    s = pl.program_id(1)

    @pl.when(jnp.logical_and(pl.program_id(0) == 0, s == 0))
    def _():
        w_ref[...] = wp_ref[...].astype(BF16)

    @pl.when(s == 0)
    def _():
        halo_ref[...] = jnp.zeros_like(halo_ref)

    x = x_ref[0]
    t = x.shape[0]
    p = _dot(x, w_ref[...])
    ext = jnp.concatenate([halo_ref[...], p], axis=0)
    halo_ref[...] = p[t - POOL_HALO:, :]
    avail = s * t + lax.broadcasted_iota(jnp.int32, (t, 1), 0) + 1
    gdim = pw_ref.shape[1]
    for g, w in enumerate(POOL_WINDOWS):
        cs = slice(g * gdim, (g + 1) * gdim)
        acc = ext[:, cs]
        sh = 1
        while sh < w:
            acc = acc + pltpu.roll(acc, sh, axis=0)
            sh *= 2
        cnt = jnp.minimum(avail, w).astype(F32)
        pooled = acc[POOL_HALO:, :] * (1.0 / cnt) - p[:, cs]
        y = _dot(pooled.astype(BF16), pw_ref[g].astype(BF16)) * sc_ref[:, cs]
        o_ref[0, :, cs] = y.astype(o_ref.dtype)


def _pool(layer, xb, w_in, col_block, pw, sc):
    bsz, seq, d = xb.shape
    _, g, gdim, _ = pw.shape
    width = g * gdim
    ts = min(TOKEN_TILE, seq)
    return pl.pallas_call(
        _pool_kernel,
        grid=(bsz, seq // ts),
        in_specs=[
            pl.BlockSpec((1, ts, d), lambda b, s: (b, s, 0)),
            pl.BlockSpec((None, d, width), lambda b, s: (layer, 0, col_block), pipeline_mode=pl.Buffered(1)),
            pl.BlockSpec((None, g, gdim, gdim), lambda b, s: (layer, 0, 0, 0)),
            pl.BlockSpec((1, width), lambda b, s: (0, 0)),
        ],
        out_specs=pl.BlockSpec((1, ts, width), lambda b, s: (b, s, 0)),
        out_shape=jax.ShapeDtypeStruct((bsz, seq, width), BF16),
        scratch_shapes=[pltpu.VMEM((d, width), BF16), pltpu.VMEM((POOL_HALO, width), F32)],
        compiler_params=_params(("arbitrary", "arbitrary")),
        name="pool",
    )(xb, w_in, pw, sc)


def _sgu_kernel(x_ref, wu_ref, wv_ref, g_ref, b_ref, ws_ref, bias_ref, o_ref, w_ref):
    @pl.when(pl.program_id(0) == 0)
    def _():
        w_ref[0] = wu_ref[...].astype(BF16)
        w_ref[1] = wv_ref[...].astype(BF16)

    x = x_ref[...]
    t = x.shape[0]
    gu = _gelu_tanh(_dot(x, w_ref[0]))
    vn = _layer_norm(_gelu_tanh(_dot(x, w_ref[1])), g_ref[...], b_ref[...]).astype(BF16)
    n_chunks = t // SG_CHUNK
    gdim = vn.shape[1] // SG_GROUPS
    ri = lax.broadcasted_iota(jnp.int32, (SG_CHUNK, SG_CHUNK), 0)
    ci = lax.broadcasted_iota(jnp.int32, (SG_CHUNK, SG_CHUNK), 1)
    for g in range(SG_GROUPS):
        cs = slice(g * gdim, (g + 1) * gdim)
        w = jnp.where(ri >= ci, ws_ref[g], 0.0).astype(BF16)
        rhs = jnp.concatenate([vn[c * SG_CHUNK:(c + 1) * SG_CHUNK, cs] for c in range(n_chunks)], axis=1)
        mixed = _dot(w, rhs)
        for c in range(n_chunks):
            rs = slice(c * SG_CHUNK, (c + 1) * SG_CHUNK)
            y = gu[rs, cs] * (mixed[:, c * gdim:(c + 1) * gdim] + bias_ref[:, cs])
            o_ref[rs, cs] = y.astype(o_ref.dtype)


def _sgu(layer, xb2, w_in, u_block, v_block, g, b, ws, bias):
    n, d = xb2.shape
    width = g.shape[1]
    ts = min(TOKEN_TILE, n)
    const2 = lambda i: (0, 0)
    return pl.pallas_call(
        _sgu_kernel,
        grid=(n // ts,),
        in_specs=[
            pl.BlockSpec((ts, d), lambda i: (i, 0)),
            pl.BlockSpec((None, d, width), lambda i: (layer, 0, u_block), pipeline_mode=pl.Buffered(1)),
            pl.BlockSpec((None, d, width), lambda i: (layer, 0, v_block), pipeline_mode=pl.Buffered(1)),
            pl.BlockSpec((1, width), const2),
            pl.BlockSpec((1, width), const2),
            pl.BlockSpec((None,) + ws.shape[1:], lambda i: (layer, 0, 0, 0)),
            pl.BlockSpec(bias.shape, const2),
        ],
        out_specs=pl.BlockSpec((ts, width), lambda i: (i, 0)),
        out_shape=jax.ShapeDtypeStruct((n, width), BF16),
        scratch_shapes=[pltpu.VMEM((2, d, width), BF16)],
        compiler_params=_params(("arbitrary",)),
        name="sgu",
    )(xb2, w_in, w_in, g, b, ws, bias)


def _merge_kernel(x_ref, yh_ref, yp_ref, ys_ref, g0_ref, g1_ref, g2_ref, p0_ref, p1_ref, p2_ref, o_ref,
                  gw_ref, pw_ref):
    @pl.when(pl.program_id(1) == 0)
    def _():
        for k, (g_ref, p_ref) in enumerate(((g0_ref, p0_ref), (g1_ref, p1_ref), (g2_ref, p2_ref))):
            gw_ref[k] = g_ref[...].astype(BF16)
            pw_ref[k] = p_ref[...].astype(BF16)

    x = x_ref[...]
    acc = None
    for k, y_ref in enumerate((yh_ref, yp_ref, ys_ref)):
        term = _sigmoid(_dot(x, gw_ref[k])) * _dot(y_ref[...], pw_ref[k])
        acc = term if acc is None else acc + term
    o_ref[...] = acc.astype(o_ref.dtype)


def _merge(layer, xb2, yh, yp, ys, w_in, gate_col0, ph, pp, ps):
    n, d = xb2.shape
    tm = min(TOKEN_TILE, n)
    tn = MERGE_TN
    gblk0 = gate_col0 // tn
    per_branch = d // tn
    width = ph.shape[1]
    assert pp.shape[1] == width and ps.shape[1] == width
    yspec = lambda y: pl.BlockSpec((tm, y.shape[1]), lambda j, i: (i, 0))
    gspec = lambda k: pl.BlockSpec((None, d, tn), lambda j, i, k=k: (layer, 0, gblk0 + k * per_branch + j))
    pspec = lambda: pl.BlockSpec((None, width, tn), lambda j, i: (layer, 0, j))
    return pl.pallas_call(
        _merge_kernel,
        grid=(d // tn, n // tm),
        in_specs=[pl.BlockSpec((tm, d), lambda j, i: (i, 0)), yspec(yh), yspec(yp), yspec(ys),
                  gspec(0), gspec(1), gspec(2), pspec(), pspec(), pspec()],
        out_specs=pl.BlockSpec((tm, tn), lambda j, i: (i, j)),
        out_shape=jax.ShapeDtypeStruct((n, d), BF16),
        scratch_shapes=[pltpu.VMEM((N_BRANCH, d, tn), BF16), pltpu.VMEM((N_BRANCH, width, tn), BF16)],
        compiler_params=_params(("arbitrary", "arbitrary")),
        name="merge",
    )(xb2, yh, yp, ys, w_in, w_in, w_in, ph, pp, ps)


def _outln_kernel(alpha, m_ref, w_ref, x_ref, g_ref, b_ref, o_ref, ob_ref, wb_ref, mix_ref):
    @pl.when(pl.program_id(0) == 0)
    def _():
        wb_ref[...] = w_ref[...].astype(BF16)

    mix_ref[...] = _dot(m_ref[...], wb_ref[...])
    _residual_layer_norm(alpha, x_ref, mix_ref, g_ref[...], b_ref[...], o_ref, ob_ref)


def _outln(alpha, layer, merged, w_out, x2, g, b):
    n, d = x2.shape
    tm = min(TOKEN_TILE // 2, n)
    const2 = lambda i: (0, 0)
    row = pl.BlockSpec((tm, d), lambda i: (i, 0))
    return pl.pallas_call(
        functools.partial(_outln_kernel, alpha),
        grid=(n // tm,),
        in_specs=[row, pl.BlockSpec((None, d, d), lambda i: (layer, 0, 0), pipeline_mode=pl.Buffered(1)), row,
                  pl.BlockSpec((1, d), const2), pl.BlockSpec((1, d), const2)],
        out_specs=[row, row],
        out_shape=[jax.ShapeDtypeStruct((n, d), F32), jax.ShapeDtypeStruct((n, d), BF16)],
        scratch_shapes=[pltpu.VMEM((d, d), BF16), pltpu.VMEM((tm, d), F32)],
        compiler_params=_params(("arbitrary",)),
        name="outln",
    )(merged, w_out, x2, g, b)


def _causal_conv(h_ref, r0, rows, c0, width, cw, cb):
    out = cb
    for k in range(CONV_W):
        tap = h_ref[pl.ds(BF16_ROWS - (CONV_W - 1) + k + r0, rows), pl.ds(c0, width)]
        out = out + cw[k:k + 1, :] * tap
    return out


def _ffn_kernel(alpha, tiles_per_seq, nf, xb_ref, xh_ref, wa_ref, wb_ref, cwa_ref, cwb_ref, cba_ref, cbb_ref,
                wd_ref, x_ref, g_ref, b_ref, o_ref, ob_ref, h_ref, gate_ref, acc_ref):
    s = pl.program_id(0)
    last = pl.num_programs(0) - 1
    ia = jnp.minimum(s, last - 1) // nf
    jb = jnp.maximum(s - 1, 0) % nf
    slot = s % 2
    tm = xb_ref.shape[0]
    tf = wa_ref.shape[1]

    @pl.when(s == 0)
    def _():
        gate_ref[...] = jnp.zeros_like(gate_ref)

    @pl.when(jb == 0)
    def _():
        acc_ref[...] = jnp.zeros_like(acc_ref)

    xb = xb_ref[...]
    xh = jnp.where(ia % tiles_per_seq == 0, jnp.zeros_like(xh_ref), xh_ref[...])
    for c0 in range(0, tf, FFN_COLS):
        for base, w_ref in ((0, wa_ref), (tf, wb_ref)):
            w = w_ref[:, pl.ds(c0, FFN_COLS)]
            h_ref[pl.ds(0, BF16_ROWS), pl.ds(base + c0, FFN_COLS)] = _dot(xh, w)
            h_ref[pl.ds(BF16_ROWS, tm), pl.ds(base + c0, FFN_COLS)] = _dot(xb, w)
        cs = pl.ds(c0, FFN_COLS)
        for r0 in range(0, tm, FFN_ROWS):
            a = _causal_conv(h_ref, r0, FFN_ROWS, c0, FFN_COLS, cwa_ref[:, cs], cba_ref[:, cs])
            b = _causal_conv(h_ref, r0, FFN_ROWS, tf + c0, FFN_COLS, cwb_ref[:, cs], cbb_ref[:, cs])
            gate_ref[slot, pl.ds(r0, FFN_ROWS), cs] = (_silu(a) * b).astype(BF16)

    acc_ref[...] += _dot(gate_ref[1 - slot], wd_ref[...])

    @pl.when(jnp.logical_and(s > 0, jb == nf - 1))
    def _():
        _residual_layer_norm(alpha, x_ref, acc_ref, g_ref[...], b_ref[...], o_ref, ob_ref)


def _ffn(alpha, seq, xb2, x2, w_up, conv_w, conv_b, w_down, g, b):
    n, d = x2.shape
    f = w_down.shape[0]
    tm = min(FFN_TM, seq)
    tf = FFN_TF
    nf = f // tf
    n_tiles = (n // tm) * nf
    halo_blocks = tm // BF16_ROWS
    up_tile = lambda s: jnp.minimum(s, n_tiles - 1)
    down_tile = lambda s: jnp.maximum(s - 1, 0)
    resid = lambda: pl.BlockSpec((tm, d), lambda s: (down_tile(s) // nf, 0))
    const2 = lambda s: (0, 0)
    return pl.pallas_call(
        functools.partial(_ffn_kernel, alpha, seq // tm, nf),
        grid=(n_tiles + 1,),
        in_specs=[
            pl.BlockSpec((tm, d), lambda s: (up_tile(s) // nf, 0)),
            pl.BlockSpec((BF16_ROWS, d), lambda s: (jnp.maximum((up_tile(s) // nf) * halo_blocks - 1, 0), 0)),
            pl.BlockSpec((d, tf), lambda s: (0, up_tile(s) % nf)),
            pl.BlockSpec((d, tf), lambda s: (0, nf + up_tile(s) % nf)),
            pl.BlockSpec((CONV_W, tf), lambda s: (0, up_tile(s) % nf)),
            pl.BlockSpec((CONV_W, tf), lambda s: (0, nf + up_tile(s) % nf)),
            pl.BlockSpec((1, tf), lambda s: (0, up_tile(s) % nf)),
            pl.BlockSpec((1, tf), lambda s: (0, nf + up_tile(s) % nf)),
            pl.BlockSpec((tf, d), lambda s: (down_tile(s) % nf, 0)),
            resid(),
            pl.BlockSpec((1, d), const2),
            pl.BlockSpec((1, d), const2),
        ],
        out_specs=[resid(), resid()],
        out_shape=[jax.ShapeDtypeStruct((n, d), F32), jax.ShapeDtypeStruct((n, d), BF16)],
        scratch_shapes=[pltpu.VMEM((BF16_ROWS + tm, 2 * tf), F32), pltpu.VMEM((2, tm, tf), BF16),
                        pltpu.VMEM((tm, d), F32)],
        compiler_params=_params(("arbitrary",)),
        name="ffn",
    )(xb2, xb2, w_up, w_up, conv_w, conv_w, conv_b, conv_b, w_down, x2, g, b)


def kernel(x, w_in, hg_lower_bounds, hg_norm_g, pool_w, pool_scale, sg_ln_g, sg_ln_b, sg_w, sg_b, w_hg_proj, w_pool_proj, w_sg_proj, w_out, ln1_g, ln1_b, w_up, conv_w, conv_b, w_down, ln2_g, ln2_b):
    bsz, seq, d = x.shape
    depth = w_in.shape[0]
    n = bsz * seq
    alpha = (2 * depth) ** 0.25
    hg_width = HG_HEADS * HG_DIM
    pool_width = pool_w.shape[1] * pool_w.shape[2]
    sg_width = sg_ln_g.shape[1]
    pool_col = 4 * hg_width
    u_col = pool_col + pool_width
    v_col = u_col + sg_width
    gate_col = v_col + sg_width
    assert pool_col % pool_width == 0 and u_col % sg_width == 0 and gate_col % MERGE_TN == 0

    x2 = x.reshape(n, d)
    xb2 = x2.astype(BF16)
    lbp = hg_lower_bounds.astype(F32)
    for l in range(depth):
        xb3 = xb2.reshape(bsz, seq, d)
        y_hg, w_up_b, w_down_b = _hgrn(l, xb3, w_in, lbp, hg_norm_g[l].reshape(1, hg_width), w_up, w_down)
        y_hg = y_hg.reshape(n, hg_width)
        y_pool = _pool(l, xb3, w_in, pool_col // pool_width, pool_w,
                       pool_scale[l].reshape(1, pool_width)).reshape(n, pool_width)
        gdim = sg_width // SG_GROUPS
        bias = jnp.repeat(sg_b[l].T, gdim, axis=1)
        y_sg = _sgu(l, xb2, w_in, u_col // sg_width, v_col // sg_width, sg_ln_g[l].reshape(1, sg_width),
                    sg_ln_b[l].reshape(1, sg_width), sg_w, bias)
        merged = _merge(l, xb2, y_hg, y_pool, y_sg, w_in, gate_col, w_hg_proj, w_pool_proj, w_sg_proj)
        x2, xb2 = _outln(alpha, l, merged, w_out, x2, ln1_g[l].reshape(1, d), ln1_b[l].reshape(1, d))
        x2, xb2 = _ffn(alpha, seq, xb2, x2, w_up_b, conv_w[l], conv_b[l].reshape(1, -1), w_down_b,
                       ln2_g[l].reshape(1, d), ln2_b[l].reshape(1, d))
    return x2.reshape(bsz, seq, d)
```

```python
import functools
import math

import jax
import jax.numpy as jnp
from jax import lax
from jax.experimental import pallas as pl
from jax.experimental.pallas import tpu as pltpu

F32 = jnp.float32
BF16 = jnp.bfloat16

HG_HEADS = 8
HG_DIM = 128
POOL_WINDOWS = (2, 4, 8, 16)
SG_GROUPS = 8
SG_CHUNK = 128
N_BRANCH = 3
CONV_W = 3
LN_EPS = 1e-5
RMS_EPS = 1e-6

LANES = 128
SUBLANES = 8
BF16_ROWS = 16
VMEM_LIMIT_BYTES = 60 * 1024 * 1024

HG_PAIR = 2 * HG_DIM
HG_CHUNK = 128
HG_LEVELS = tuple(HG_CHUNK >> (i + 1) for i in range(int(math.log2(HG_CHUNK))))
POOL_HALO = 16
TOKEN_TILE = 512
MERGE_TN = 512
FFN_TM = 512
FFN_TF = 512
LN_ROWS = 128


def _params(semantics):
    return pltpu.CompilerParams(dimension_semantics=semantics, vmem_limit_bytes=VMEM_LIMIT_BYTES)


def _dot(a, b):
    return jnp.dot(a, b, preferred_element_type=F32)


def _dot_nt(a, b):
    return lax.dot_general(a, b, (((1,), (1,)), ((), ())), preferred_element_type=F32)


def _dot_tn(a, b):
    return lax.dot_general(a, b, (((0,), (0,)), ((), ())), preferred_element_type=F32)


def _sigmoid(x):
    return 1.0 / (1.0 + jnp.exp(-x))


def _silu(x):
    return x * _sigmoid(x)


def _gelu_tanh(x):
    c = math.sqrt(2.0 / math.pi)
    return 0.5 * x * (1.0 + jnp.tanh(c * (x + 0.044715 * (x * x * x))))


def _layer_norm(y, g, b):
    mu = jnp.mean(y, axis=-1, keepdims=True)
    d = y - mu
    var = jnp.mean(d * d, axis=-1, keepdims=True)
    return d * lax.rsqrt(var + LN_EPS) * g + b


def _residual_layer_norm(alpha, x_ref, y_ref, g, b, o_ref, ob_ref):
    def body(i, carry):
        rs = pl.ds(pl.multiple_of(i * LN_ROWS, LN_ROWS), LN_ROWS)
        out = _layer_norm(alpha * x_ref[rs, :] + y_ref[rs, :], g, b)
        o_ref[rs, :] = out
        ob_ref[rs, :] = out.astype(ob_ref.dtype)
        return carry
    lax.fori_loop(0, x_ref.shape[0] // LN_ROWS, body, 0)


def _block_reference(b, h, row):
    t, c = b.shape
    if 2 * h >= SUBLANES:
        blk = b.reshape(t // (2 * h), 2 * h, c)
        ref = jnp.broadcast_to(blk[:, h - 1:h, :], blk.shape)
        return ref.reshape(t, c)
    pos = row & (2 * h - 1)
    out = b
    for p in range(2 * h):
        off = p - (h - 1)
        if off == 0:
            continue
        shifted = pltpu.roll(b, off % t, axis=0)
        out = jnp.where(pos == p, shifted, out)
    return out


def _hgrn_kernel(layer, x_ref, wq_ref, wf_ref, wi_ref, wg_ref, lbp_ref, ng_ref, up_ref, down_ref,
                 o_ref, upb_ref, downb_ref, w_ref, state_ref):
    upb_ref[...] = up_ref[...].astype(BF16)
    downb_ref[...] = down_ref[...].astype(BF16)

    @pl.when(pl.program_id(2) == 0)
    def _():
        state_ref[...] = jnp.zeros_like(state_ref)
        for k, src in enumerate((wq_ref, wf_ref, wi_ref, wg_ref)):
            w_ref[k] = src[...].astype(BF16)

    x = x_ref[0]
    t = x.shape[0]
    zq = _dot(x, w_ref[0])
    zf = _dot(x, w_ref[1])
    zi = _dot(x, w_ref[2])
    zg = _dot(x, w_ref[3])

    qf = _silu(zq)
    e = jnp.exp(-jnp.abs(zf))
    inv1pe = 1.0 / (1.0 + e)
    log_sig = jnp.minimum(zf, 0.0) - jnp.log(1.0 + e)
    sig_neg = jnp.where(zf >= 0.0, e * inv1pe, inv1pe)
    if layer == 0:
        lf = log_sig
        kf = sig_neg
    else:
        p = lbp_ref[...]
        rows = [p[i:i + 1, :] for i in range(p.shape[0])]
        m = functools.reduce(jnp.maximum, rows)
        ex = [jnp.exp(r - m) for r in rows]
        lb = functools.reduce(jnp.add, ex[1:layer + 1]) / functools.reduce(jnp.add, ex)
        la = jnp.log(lb)
        lbv = jnp.log(1.0 - lb) + log_sig
        lf = jnp.maximum(la, lbv) + jnp.log(1.0 + jnp.exp(-jnp.abs(la - lbv)))
        kf = (1.0 - lb) * sig_neg

    row = lax.broadcasted_iota(jnp.int32, (t, 1), 0)
    pos = row & (HG_CHUNK - 1)
    b = lf
    sh = 1
    while sh < HG_CHUNK:
        b = b + jnp.where(pos >= sh, pltpu.roll(b, sh, axis=0), 0.0)
        sh *= 2

    n_chunks = t // HG_CHUNK
    n_heads = HG_PAIR // HG_DIM
    ti = lax.broadcasted_iota(jnp.int32, (HG_CHUNK, HG_CHUNK), 0)
    si = lax.broadcasted_iota(jnp.int32, (HG_CHUNK, HG_CHUNK), 1)

    scores = [[None] * n_heads for _ in range(n_chunks)]
    for h in HG_LEVELS:
        ref = _block_reference(b, h, row)
        w = jnp.exp(-jnp.abs(b - ref))
        second = (row & h) != 0
        qt = jnp.where(second, qf * w, 0.0).astype(BF16)
        kt = jnp.where(second, 0.0, kf * w).astype(BF16)
        same = (ti // (2 * h)) == (si // (2 * h))
        for c in range(n_chunks):
            r0 = c * HG_CHUNK
            for hh in range(n_heads):
                c0 = hh * HG_DIM
                a = _dot_nt(qt[r0:r0 + HG_CHUNK, c0:c0 + HG_DIM], kt[r0:r0 + HG_CHUNK, c0:c0 + HG_DIM])
                a = jnp.where(same, a, 0.0)
                scores[c][hh] = a if scores[c][hh] is None else scores[c][hh] + a

    vb = zi.astype(BF16)
    qk = qf * kf
    qd = (qf * jnp.exp(b)).astype(BF16)
    ng = ng_ref[...]
    gate = _silu(zg)
    for hh in range(n_heads):
        c0 = hh * HG_DIM
        st = state_ref[hh]
        for c in range(n_chunks):
            r0 = c * HG_CHUNK
            rs = slice(r0, r0 + HG_CHUNK)
            cs = slice(c0, c0 + HG_DIM)
            b_c = b[rs, cs]
            b_last = b_c[HG_CHUNK - 1:HG_CHUNK, :]
            v_c = vb[rs, cs]
            o = _dot(scores[c][hh].astype(BF16), v_c)
            o = o + jnp.sum(qk[rs, cs], axis=-1, keepdims=True) * zi[rs, cs]
            o = o + _dot_nt(qd[rs, cs], st.astype(BF16))
            kd = (kf[rs, cs] * jnp.exp(b_last - b_c)).astype(BF16)
            st = jnp.exp(b_last) * st + _dot_tn(v_c, kd)
            o = o * lax.rsqrt(jnp.mean(o * o, axis=-1, keepdims=True) + RMS_EPS)
            o = o * ng[:, cs] * gate[rs, cs]
            o_ref[0, rs, cs] = o.astype(o_ref.dtype)
        state_ref[hh] = st


def _hgrn(layer, xb, w_in, lbp, ng, w_up, w_down):
    bsz, seq, d = xb.shape
    width = HG_HEADS * HG_DIM
    n_pairs = width // HG_PAIR
    ts = min(TOKEN_TILE, seq)
    n_s = seq // ts
    n_steps = bsz * n_pairs * n_s

    def slab_rows(w):
        rows = w.shape[1]
        r = next(r for r in range(BF16_ROWS, rows + 1, BF16_ROWS) if rows % r == 0 and rows // r <= n_steps)
        return r, rows // r

    step = lambda b, p, s: (b * n_pairs + p) * n_s + s

    def slab_specs(w):
        r, n_slabs = slab_rows(w)
        idx = lambda b, p, s: jnp.minimum(step(b, p, s), n_slabs - 1)
        return (pl.BlockSpec((None, r, w.shape[2]), lambda b, p, s: (layer, idx(b, p, s), 0)),
                pl.BlockSpec((r, w.shape[2]), lambda b, p, s: (idx(b, p, s), 0)))

    (up_in, up_out), (down_in, down_out) = slab_specs(w_up), slab_specs(w_down)
    wspec = lambda k: pl.BlockSpec((None, d, HG_PAIR), lambda b, p, s, k=k: (layer, 0, k * n_pairs + p))
    return pl.pallas_call(
        functools.partial(_hgrn_kernel, layer),
        grid=(bsz, n_pairs, n_s),
        in_specs=[
            pl.BlockSpec((1, ts, d), lambda b, p, s: (b, s, 0)),
            wspec(0), wspec(1), wspec(2), wspec(3),
            pl.BlockSpec((lbp.shape[0], HG_PAIR), lambda b, p, s: (0, p)),
            pl.BlockSpec((1, HG_PAIR), lambda b, p, s: (0, p)),
            up_in, down_in,
        ],
        out_specs=[pl.BlockSpec((1, ts, HG_PAIR), lambda b, p, s: (b, s, p)), up_out, down_out],
        out_shape=[jax.ShapeDtypeStruct((bsz, seq, width), BF16),
                   jax.ShapeDtypeStruct(w_up.shape[1:], BF16), jax.ShapeDtypeStruct(w_down.shape[1:], BF16)],
        scratch_shapes=[pltpu.VMEM((4, d, HG_PAIR), BF16), pltpu.VMEM((HG_PAIR // HG_DIM, HG_DIM, HG_DIM), F32)],
        compiler_params=_params(("arbitrary", "arbitrary", "arbitrary")),
        name="hgrn",
    )(xb, w_in, w_in, w_in, w_in, lbp, ng, w_up, w_down)


def _pool_kernel(x_ref, wp_ref, pw_ref, sc_ref, o_ref, w_ref, halo_ref):
    s = pl.program_id(1)

    @pl.when(jnp.logical_and(pl.program_id(0) == 0, s == 0))
    def _():
        w_ref[...] = wp_ref[...].astype(BF16)

    @pl.when(s == 0)
    def _():
        halo_ref[...] = jnp.zeros_like(halo_ref)

    x = x_ref[0]
    t = x.shape[0]
    p = _dot(x, w_ref[...])
    ext = jnp.concatenate([halo_ref[...], p], axis=0)
    halo_ref[...] = p[t - POOL_HALO:, :]
    avail = s * t + lax.broadcasted_iota(jnp.int32, (t, 1), 0) + 1
    gdim = pw_ref.shape[1]
    for g, w in enumerate(POOL_WINDOWS):
        cs = slice(g * gdim, (g + 1) * gdim)
        acc = ext[:, cs]
        sh = 1
        while sh < w:
            acc = acc + pltpu.roll(acc, sh, axis=0)
            sh *= 2
        cnt = jnp.minimum(avail, w).astype(F32)
        pooled = acc[POOL_HALO:, :] * (1.0 / cnt) - p[:, cs]
        y = _dot(pooled.astype(BF16), pw_ref[g].astype(BF16)) * sc_ref[:, cs]
        o_ref[0, :, cs] = y.astype(o_ref.dtype)


def _pool(layer, xb, w_in, col_block, pw, sc):
    bsz, seq, d = xb.shape
    _, g, gdim, _ = pw.shape
    width = g * gdim
    ts = min(TOKEN_TILE, seq)
    return pl.pallas_call(
        _pool_kernel,
        grid=(bsz, seq // ts),
        in_specs=[
            pl.BlockSpec((1, ts, d), lambda b, s: (b, s, 0)),
            pl.BlockSpec((None, d, width), lambda b, s: (layer, 0, col_block), pipeline_mode=pl.Buffered(1)),
            pl.BlockSpec((None, g, gdim, gdim), lambda b, s: (layer, 0, 0, 0)),
            pl.BlockSpec((1, width), lambda b, s: (0, 0)),
        ],
        out_specs=pl.BlockSpec((1, ts, width), lambda b, s: (b, s, 0)),
        out_shape=jax.ShapeDtypeStruct((bsz, seq, width), BF16),
        scratch_shapes=[pltpu.VMEM((d, width), BF16), pltpu.VMEM((POOL_HALO, width), F32)],
        compiler_params=_params(("arbitrary", "arbitrary")),
        name="pool",
    )(xb, w_in, pw, sc)


def _sgu_kernel(x_ref, wu_ref, wv_ref, g_ref, b_ref, ws_ref, bias_ref, o_ref, w_ref):
    @pl.when(pl.program_id(0) == 0)
    def _():
        w_ref[0] = wu_ref[...].astype(BF16)
        w_ref[1] = wv_ref[...].astype(BF16)

    x = x_ref[...]
    t = x.shape[0]
    gu = _gelu_tanh(_dot(x, w_ref[0]))
    vn = _layer_norm(_gelu_tanh(_dot(x, w_ref[1])), g_ref[...], b_ref[...]).astype(BF16)
    n_chunks = t // SG_CHUNK
    gdim = vn.shape[1] // SG_GROUPS
    ri = lax.broadcasted_iota(jnp.int32, (SG_CHUNK, SG_CHUNK), 0)
    ci = lax.broadcasted_iota(jnp.int32, (SG_CHUNK, SG_CHUNK), 1)
    for g in range(SG_GROUPS):
        cs = slice(g * gdim, (g + 1) * gdim)
        w = jnp.where(ri >= ci, ws_ref[g], 0.0).astype(BF16)
        rhs = jnp.concatenate([vn[c * SG_CHUNK:(c + 1) * SG_CHUNK, cs] for c in range(n_chunks)], axis=1)
        mixed = _dot(w, rhs)
        for c in range(n_chunks):
            rs = slice(c * SG_CHUNK, (c + 1) * SG_CHUNK)
            y = gu[rs, cs] * (mixed[:, c * gdim:(c + 1) * gdim] + bias_ref[:, cs])
            o_ref[rs, cs] = y.astype(o_ref.dtype)


def _sgu(layer, xb2, w_in, u_block, v_block, g, b, ws, bias):
    n, d = xb2.shape
    width = g.shape[1]
    ts = min(TOKEN_TILE, n)
    const2 = lambda i: (0, 0)
    return pl.pallas_call(
        _sgu_kernel,
        grid=(n // ts,),
        in_specs=[
            pl.BlockSpec((ts, d), lambda i: (i, 0)),
            pl.BlockSpec((None, d, width), lambda i: (layer, 0, u_block), pipeline_mode=pl.Buffered(1)),
            pl.BlockSpec((None, d, width), lambda i: (layer, 0, v_block), pipeline_mode=pl.Buffered(1)),
            pl.BlockSpec((1, width), const2),
            pl.BlockSpec((1, width), const2),
            pl.BlockSpec((None,) + ws.shape[1:], lambda i: (layer, 0, 0, 0)),
            pl.BlockSpec(bias.shape, const2),
        ],
        out_specs=pl.BlockSpec((ts, width), lambda i: (i, 0)),
        out_shape=jax.ShapeDtypeStruct((n, width), BF16),
        scratch_shapes=[pltpu.VMEM((2, d, width), BF16)],
        compiler_params=_params(("arbitrary",)),
        name="sgu",
    )(xb2, w_in, w_in, g, b, ws, bias)


def _merge_kernel(x_ref, yh_ref, yp_ref, ys_ref, g0_ref, g1_ref, g2_ref, p0_ref, p1_ref, p2_ref, o_ref,
                  gw_ref, pw_ref):
    @pl.when(pl.program_id(1) == 0)
    def _():
        for k, (g_ref, p_ref) in enumerate(((g0_ref, p0_ref), (g1_ref, p1_ref), (g2_ref, p2_ref))):
            gw_ref[k] = g_ref[...].astype(BF16)
            pw_ref[k] = p_ref[...].astype(BF16)

    x = x_ref[...]
    acc = None
    for k, y_ref in enumerate((yh_ref, yp_ref, ys_ref)):
        term = _sigmoid(_dot(x, gw_ref[k])) * _dot(y_ref[...], pw_ref[k])
        acc = term if acc is None else acc + term
    o_ref[...] = acc.astype(o_ref.dtype)


def _merge(layer, xb2, yh, yp, ys, w_in, gate_col0, ph, pp, ps):
    n, d = xb2.shape
    tm = min(TOKEN_TILE, n)
    tn = MERGE_TN
    gblk0 = gate_col0 // tn
    per_branch = d // tn
    width = ph.shape[1]
    assert pp.shape[1] == width and ps.shape[1] == width
    once = pl.Buffered(1)
    yspec = lambda y: pl.BlockSpec((tm, y.shape[1]), lambda j, i: (i, 0))
    gspec = lambda k: pl.BlockSpec((None, d, tn), lambda j, i, k=k: (layer, 0, gblk0 + k * per_branch + j),
                                   pipeline_mode=once)
    pspec = lambda: pl.BlockSpec((None, width, tn), lambda j, i: (layer, 0, j), pipeline_mode=once)
    return pl.pallas_call(
        _merge_kernel,
        grid=(d // tn, n // tm),
        in_specs=[pl.BlockSpec((tm, d), lambda j, i: (i, 0)), yspec(yh), yspec(yp), yspec(ys),
                  gspec(0), gspec(1), gspec(2), pspec(), pspec(), pspec()],
        out_specs=pl.BlockSpec((tm, tn), lambda j, i: (i, j)),
        out_shape=jax.ShapeDtypeStruct((n, d), BF16),
        scratch_shapes=[pltpu.VMEM((N_BRANCH, d, tn), BF16), pltpu.VMEM((N_BRANCH, width, tn), BF16)],
        compiler_params=_params(("arbitrary", "arbitrary")),
        name="merge",
    )(xb2, yh, yp, ys, w_in, w_in, w_in, ph, pp, ps)


def _outln_kernel(alpha, m_ref, w_ref, x_ref, g_ref, b_ref, o_ref, ob_ref, wb_ref, mix_ref):
    @pl.when(pl.program_id(0) == 0)
    def _():
        wb_ref[...] = w_ref[...].astype(BF16)

    mix_ref[...] = _dot(m_ref[...], wb_ref[...])
    _residual_layer_norm(alpha, x_ref, mix_ref, g_ref[...], b_ref[...], o_ref, ob_ref)


def _outln(alpha, layer, merged, w_out, x2, g, b):
    n, d = x2.shape
    tm = min(TOKEN_TILE // 2, n)
    const2 = lambda i: (0, 0)
    row = pl.BlockSpec((tm, d), lambda i: (i, 0))
    return pl.pallas_call(
        functools.partial(_outln_kernel, alpha),
        grid=(n // tm,),
        in_specs=[row, pl.BlockSpec((None, d, d), lambda i: (layer, 0, 0), pipeline_mode=pl.Buffered(1)), row,
                  pl.BlockSpec((1, d), const2), pl.BlockSpec((1, d), const2)],
        out_specs=[row, row],
        out_shape=[jax.ShapeDtypeStruct((n, d), F32), jax.ShapeDtypeStruct((n, d), BF16)],
        scratch_shapes=[pltpu.VMEM((d, d), BF16), pltpu.VMEM((tm, d), F32)],
        compiler_params=_params(("arbitrary",)),
        name="outln",
    )(merged, w_out, x2, g, b)


def _causal_conv(h, halo, cw, cb, row):
    n = halo.shape[0]
    h1 = jnp.where(row == 0, halo[n - 1:n, :], pltpu.roll(h, 1, axis=0))
    h2 = jnp.where(row == 0, halo[n - 2:n - 1, :], jnp.where(row == 1, halo[n - 1:n, :], pltpu.roll(h, 2, axis=0)))
    return cb + cw[0:1, :] * h2 + cw[1:2, :] * h1 + cw[2:3, :] * h


def _ffn_kernel(alpha, tiles_per_seq, xb_ref, xh_ref, wa_ref, wb_ref, cwa_ref, cwb_ref, cba_ref, cbb_ref,
                wd_ref, x_ref, g_ref, b_ref, o_ref, ob_ref, acc_ref):
    i = pl.program_id(0)
    j = pl.program_id(1)

    @pl.when(j == 0)
    def _():
        acc_ref[...] = jnp.zeros_like(acc_ref)

    xb = xb_ref[...]
    xh = jnp.where(i % tiles_per_seq == 0, jnp.zeros_like(xh_ref), xh_ref[...])
    row = lax.broadcasted_iota(jnp.int32, (xb.shape[0], 1), 0)
    wa = wa_ref[...]
    wb = wb_ref[...]
    a = _causal_conv(_dot(xb, wa), _dot(xh, wa), cwa_ref[...], cba_ref[...], row)
    b = _causal_conv(_dot(xb, wb), _dot(xh, wb), cwb_ref[...], cbb_ref[...], row)
    acc_ref[...] += _dot((_silu(a) * b).astype(BF16), wd_ref[...])

    @pl.when(j == pl.num_programs(1) - 1)
    def _():
        _residual_layer_norm(alpha, x_ref, acc_ref, g_ref[...], b_ref[...], o_ref, ob_ref)


def _ffn(alpha, seq, xb2, x2, w_up, conv_w, conv_b, w_down, g, b):
    n, d = x2.shape
    f = w_down.shape[0]
    tm = min(FFN_TM, seq)
    tf = FFN_TF
    nf = f // tf
    halo_blocks = tm // BF16_ROWS
    row = pl.BlockSpec((tm, d), lambda i, j: (i, 0))
    const2 = lambda i, j: (0, 0)
    return pl.pallas_call(
        functools.partial(_ffn_kernel, alpha, seq // tm),
        grid=(n // tm, nf),
        in_specs=[
            row,
            pl.BlockSpec((BF16_ROWS, d), lambda i, j: (jnp.maximum(i * halo_blocks - 1, 0), 0)),
            pl.BlockSpec((d, tf), lambda i, j: (0, j)),
            pl.BlockSpec((d, tf), lambda i, j: (0, nf + j)),
            pl.BlockSpec((CONV_W, tf), lambda i, j: (0, j)),
            pl.BlockSpec((CONV_W, tf), lambda i, j: (0, nf + j)),
            pl.BlockSpec((1, tf), lambda i, j: (0, j)),
            pl.BlockSpec((1, tf), lambda i, j: (0, nf + j)),
            pl.BlockSpec((tf, d), lambda i, j: (j, 0)),
            row,
            pl.BlockSpec((1, d), const2),
            pl.BlockSpec((1, d), const2),
        ],
        out_specs=[row, row],
        out_shape=[jax.ShapeDtypeStruct((n, d), F32), jax.ShapeDtypeStruct((n, d), BF16)],
        scratch_shapes=[pltpu.VMEM((tm, d), F32)],
        compiler_params=_params(("parallel", "arbitrary")),
        name="ffn",
    )(xb2, xb2, w_up, w_up, conv_w, conv_w, conv_b, conv_b, w_down, x2, g, b)


def kernel(x, w_in, hg_lower_bounds, hg_norm_g, pool_w, pool_scale, sg_ln_g, sg_ln_b, sg_w, sg_b, w_hg_proj, w_pool_proj, w_sg_proj, w_out, ln1_g, ln1_b, w_up, conv_w, conv_b, w_down, ln2_g, ln2_b):
    bsz, seq, d = x.shape
    depth = w_in.shape[0]
    n = bsz * seq
    alpha = (2 * depth) ** 0.25
    hg_width = HG_HEADS * HG_DIM
    pool_width = pool_w.shape[1] * pool_w.shape[2]
    sg_width = sg_ln_g.shape[1]
    pool_col = 4 * hg_width
    u_col = pool_col + pool_width
    v_col = u_col + sg_width
    gate_col = v_col + sg_width
    assert pool_col % pool_width == 0 and u_col % sg_width == 0 and gate_col % MERGE_TN == 0

    x2 = x.reshape(n, d)
    xb2 = x2.astype(BF16)
    lbp = hg_lower_bounds.astype(F32)
    for l in range(depth):
        xb3 = xb2.reshape(bsz, seq, d)
        y_hg, w_up_b, w_down_b = _hgrn(l, xb3, w_in, lbp, hg_norm_g[l].reshape(1, hg_width), w_up, w_down)
        y_hg = y_hg.reshape(n, hg_width)
        y_pool = _pool(l, xb3, w_in, pool_col // pool_width, pool_w,
                       pool_scale[l].reshape(1, pool_width)).reshape(n, pool_width)
        gdim = sg_width // SG_GROUPS
        bias = jnp.repeat(sg_b[l].T, gdim, axis=1)
        y_sg = _sgu(l, xb2, w_in, u_col // sg_width, v_col // sg_width, sg_ln_g[l].reshape(1, sg_width),
                    sg_ln_b[l].reshape(1, sg_width), sg_w, bias)
        merged = _merge(l, xb2, y_hg, y_pool, y_sg, w_in, gate_col, w_hg_proj, w_pool_proj, w_sg_proj)
        x2, xb2 = _outln(alpha, l, merged, w_out, x2, ln1_g[l].reshape(1, d), ln1_b[l].reshape(1, d))
        x2, xb2 = _ffn(alpha, seq, xb2, x2, w_up_b, conv_w[l], conv_b[l].reshape(1, -1), w_down_b,
                       ln2_g[l].reshape(1, d), ln2_b[l].reshape(1, d))
    return x2.reshape(bsz, seq, d)
```

```python
import functools
import math

import jax
import jax.numpy as jnp
from jax import lax
from jax.experimental import pallas as pl
from jax.experimental.pallas import tpu as pltpu

F32 = jnp.float32
BF16 = jnp.bfloat16

HG_HEADS = 8
HG_DIM = 128
POOL_WINDOWS = (2, 4, 8, 16)
SG_GROUPS = 8
SG_CHUNK = 128
N_BRANCH = 3
CONV_W = 3
LN_EPS = 1e-5
RMS_EPS = 1e-6

LANES = 128
SUBLANES = 8
BF16_ROWS = 16
VMEM_LIMIT_BYTES = 60 * 1024 * 1024

HG_PAIR = 2 * HG_DIM
HG_CHUNK = 128
HG_LEVELS = tuple(HG_CHUNK >> (i + 1) for i in range(int(math.log2(HG_CHUNK))))
POOL_HALO = 16
TOKEN_TILE = 512
MERGE_TN = 512
FFN_TM = 512
FFN_TF = 512
FFN_COLS = 256
LN_ROWS = 128


def _params(semantics):
    return pltpu.CompilerParams(dimension_semantics=semantics, vmem_limit_bytes=VMEM_LIMIT_BYTES)


def _dot(a, b):
    return jnp.dot(a, b, preferred_element_type=F32)


def _dot_nt(a, b):
    return lax.dot_general(a, b, (((1,), (1,)), ((), ())), preferred_element_type=F32)


def _dot_tn(a, b):
    return lax.dot_general(a, b, (((0,), (0,)), ((), ())), preferred_element_type=F32)


def _sigmoid(x):
    return 1.0 / (1.0 + jnp.exp(-x))


def _silu(x):
    return x * _sigmoid(x)


def _gelu_tanh(x):
    c = math.sqrt(2.0 / math.pi)
    return 0.5 * x * (1.0 + jnp.tanh(c * (x + 0.044715 * (x * x * x))))


def _layer_norm(y, g, b):
    mu = jnp.mean(y, axis=-1, keepdims=True)
    d = y - mu
    var = jnp.mean(d * d, axis=-1, keepdims=True)
    return d * lax.rsqrt(var + LN_EPS) * g + b


def _residual_layer_norm(alpha, x_ref, y_ref, g, b, o_ref, ob_ref):
    def body(i, carry):
        rs = pl.ds(pl.multiple_of(i * LN_ROWS, LN_ROWS), LN_ROWS)
        out = _layer_norm(alpha * x_ref[rs, :] + y_ref[rs, :], g, b)
        o_ref[rs, :] = out
        ob_ref[rs, :] = out.astype(ob_ref.dtype)
        return carry
    lax.fori_loop(0, x_ref.shape[0] // LN_ROWS, body, 0)


def _block_reference(b, h, row):
    t, c = b.shape
    if 2 * h >= SUBLANES:
        blk = b.reshape(t // (2 * h), 2 * h, c)
        ref = jnp.broadcast_to(blk[:, h - 1:h, :], blk.shape)
        return ref.reshape(t, c)
    pos = row & (2 * h - 1)
    out = b
    for p in range(2 * h):
        off = p - (h - 1)
        if off == 0:
            continue
        shifted = pltpu.roll(b, off % t, axis=0)
        out = jnp.where(pos == p, shifted, out)
    return out


def _hgrn_kernel(layer, x_ref, wq_ref, wf_ref, wi_ref, wg_ref, lbp_ref, ng_ref, up_ref, down_ref,
                 o_ref, upb_ref, downb_ref, w_ref, state_ref):
    upb_ref[...] = up_ref[...].astype(BF16)
    downb_ref[...] = down_ref[...].astype(BF16)

    @pl.when(pl.program_id(2) == 0)
    def _():
        state_ref[...] = jnp.zeros_like(state_ref)
        for k, src in enumerate((wq_ref, wf_ref, wi_ref, wg_ref)):
            w_ref[k] = src[...].astype(BF16)

    x = x_ref[0]
    t = x.shape[0]
    zq = _dot(x, w_ref[0])
    zf = _dot(x, w_ref[1])
    zi = _dot(x, w_ref[2])
    zg = _dot(x, w_ref[3])

    qf = _silu(zq)
    e = jnp.exp(-jnp.abs(zf))
    inv1pe = 1.0 / (1.0 + e)
    log_sig = jnp.minimum(zf, 0.0) - jnp.log(1.0 + e)
    sig_neg = jnp.where(zf >= 0.0, e * inv1pe, inv1pe)
    if layer == 0:
        lf = log_sig
        kf = sig_neg
    else:
        p = lbp_ref[...]
        rows = [p[i:i + 1, :] for i in range(p.shape[0])]
        m = functools.reduce(jnp.maximum, rows)
        ex = [jnp.exp(r - m) for r in rows]
        lb = functools.reduce(jnp.add, ex[1:layer + 1]) / functools.reduce(jnp.add, ex)
        la = jnp.log(lb)
        lbv = jnp.log(1.0 - lb) + log_sig
        lf = jnp.maximum(la, lbv) + jnp.log(1.0 + jnp.exp(-jnp.abs(la - lbv)))
        kf = (1.0 - lb) * sig_neg

    row = lax.broadcasted_iota(jnp.int32, (t, 1), 0)
    pos = row & (HG_CHUNK - 1)
    b = lf
    sh = 1
    while sh < HG_CHUNK:
        b = b + jnp.where(pos >= sh, pltpu.roll(b, sh, axis=0), 0.0)
        sh *= 2

    n_chunks = t // HG_CHUNK
    n_heads = HG_PAIR // HG_DIM
    ti = lax.broadcasted_iota(jnp.int32, (HG_CHUNK, HG_CHUNK), 0)
    si = lax.broadcasted_iota(jnp.int32, (HG_CHUNK, HG_CHUNK), 1)

    scores = [[None] * n_heads for _ in range(n_chunks)]
    for h in HG_LEVELS:
        ref = _block_reference(b, h, row)
        w = jnp.exp(-jnp.abs(b - ref))
        second = (row & h) != 0
        qt = jnp.where(second, qf * w, 0.0).astype(BF16)
        kt = jnp.where(second, 0.0, kf * w).astype(BF16)
        same = (ti // (2 * h)) == (si // (2 * h))
        for c in range(n_chunks):
            r0 = c * HG_CHUNK
            for hh in range(n_heads):
                c0 = hh * HG_DIM
                a = _dot_nt(qt[r0:r0 + HG_CHUNK, c0:c0 + HG_DIM], kt[r0:r0 + HG_CHUNK, c0:c0 + HG_DIM])
                a = jnp.where(same, a, 0.0)
                scores[c][hh] = a if scores[c][hh] is None else scores[c][hh] + a

    vb = zi.astype(BF16)
    qk = qf * kf
    qd = (qf * jnp.exp(b)).astype(BF16)
    ng = ng_ref[...]
    gate = _silu(zg)
    for hh in range(n_heads):
        c0 = hh * HG_DIM
        st = state_ref[hh]
        for c in range(n_chunks):
            r0 = c * HG_CHUNK
            rs = slice(r0, r0 + HG_CHUNK)
            cs = slice(c0, c0 + HG_DIM)
            b_c = b[rs, cs]
            b_last = b_c[HG_CHUNK - 1:HG_CHUNK, :]
            v_c = vb[rs, cs]
            o = _dot(scores[c][hh].astype(BF16), v_c)
            o = o + jnp.sum(qk[rs, cs], axis=-1, keepdims=True) * zi[rs, cs]
            o = o + _dot_nt(qd[rs, cs], st.astype(BF16))
            kd = (kf[rs, cs] * jnp.exp(b_last - b_c)).astype(BF16)
            st = jnp.exp(b_last) * st + _dot_tn(v_c, kd)
            o = o * lax.rsqrt(jnp.mean(o * o, axis=-1, keepdims=True) + RMS_EPS)
            o = o * ng[:, cs] * gate[rs, cs]
            o_ref[0, rs, cs] = o.astype(o_ref.dtype)
        state_ref[hh] = st


def _hgrn(layer, xb, w_in, lbp, ng, w_up, w_down):
    bsz, seq, d = xb.shape
    width = HG_HEADS * HG_DIM
    n_pairs = width // HG_PAIR
    ts = min(TOKEN_TILE, seq)
    n_s = seq // ts
    n_steps = bsz * n_pairs * n_s

    def slab_rows(w):
        rows = w.shape[1]
        r = next(r for r in range(BF16_ROWS, rows + 1, BF16_ROWS) if rows % r == 0 and rows // r <= n_steps)
        return r, rows // r

    step = lambda b, p, s: (b * n_pairs + p) * n_s + s

    def slab_specs(w):
        r, n_slabs = slab_rows(w)
        idx = lambda b, p, s: jnp.minimum(step(b, p, s), n_slabs - 1)
        return (pl.BlockSpec((None, r, w.shape[2]), lambda b, p, s: (layer, idx(b, p, s), 0)),
                pl.BlockSpec((r, w.shape[2]), lambda b, p, s: (idx(b, p, s), 0)))

    (up_in, up_out), (down_in, down_out) = slab_specs(w_up), slab_specs(w_down)
    wspec = lambda k: pl.BlockSpec((None, d, HG_PAIR), lambda b, p, s, k=k: (layer, 0, k * n_pairs + p))
    return pl.pallas_call(
        functools.partial(_hgrn_kernel, layer),
        grid=(bsz, n_pairs, n_s),
        in_specs=[
            pl.BlockSpec((1, ts, d), lambda b, p, s: (b, s, 0)),
            wspec(0), wspec(1), wspec(2), wspec(3),
            pl.BlockSpec((lbp.shape[0], HG_PAIR), lambda b, p, s: (0, p)),
            pl.BlockSpec((1, HG_PAIR), lambda b, p, s: (0, p)),
            up_in, down_in,
        ],
        out_specs=[pl.BlockSpec((1, ts, HG_PAIR), lambda b, p, s: (b, s, p)), up_out, down_out],
        out_shape=[jax.ShapeDtypeStruct((bsz, seq, width), BF16),
                   jax.ShapeDtypeStruct(w_up.shape[1:], BF16), jax.ShapeDtypeStruct(w_down.shape[1:], BF16)],
        scratch_shapes=[pltpu.VMEM((4, d, HG_PAIR), BF16), pltpu.VMEM((HG_PAIR // HG_DIM, HG_DIM, HG_DIM), F32)],
        compiler_params=_params(("arbitrary", "arbitrary", "arbitrary")),
        name="hgrn",
    )(xb, w_in, w_in, w_in, w_in, lbp, ng, w_up, w_down)


def _pool_kernel(x_ref, wp_ref, pw_ref, sc_ref, o_ref, w_ref, halo_ref):
    s = pl.program_id(1)

    @pl.when(jnp.logical_and(pl.program_id(0) == 0, s == 0))
    def _():
        w_ref[...] = wp_ref[...].astype(BF16)

    @pl.when(s == 0)
    def _():
        halo_ref[...] = jnp.zeros_like(halo_ref)

    x = x_ref[0]
    t = x.shape[0]
    p = _dot(x, w_ref[...])
    ext = jnp.concatenate([halo_ref[...], p], axis=0)
    halo_ref[...] = p[t - POOL_HALO:, :]
    avail = s * t + lax.broadcasted_iota(jnp.int32, (t, 1), 0) + 1
    gdim = pw_ref.shape[1]
    for g, w in enumerate(POOL_WINDOWS):
        cs = slice(g * gdim, (g + 1) * gdim)
        acc = ext[:, cs]
        sh = 1
        while sh < w:
            acc = acc + pltpu.roll(acc, sh, axis=0)
            sh *= 2
        cnt = jnp.minimum(avail, w).astype(F32)
        pooled = acc[POOL_HALO:, :] * (1.0 / cnt) - p[:, cs]
        y = _dot(pooled.astype(BF16), pw_ref[g].astype(BF16)) * sc_ref[:, cs]
        o_ref[0, :, cs] = y.astype(o_ref.dtype)


def _pool(layer, xb, w_in, col_block, pw, sc):
    bsz, seq, d = xb.shape
    _, g, gdim, _ = pw.shape
    width = g * gdim
    ts = min(TOKEN_TILE, seq)
    return pl.pallas_call(
        _pool_kernel,
        grid=(bsz, seq // ts),
        in_specs=[
            pl.BlockSpec((1, ts, d), lambda b, s: (b, s, 0)),
            pl.BlockSpec((None, d, width), lambda b, s: (layer, 0, col_block), pipeline_mode=pl.Buffered(1)),
            pl.BlockSpec((None, g, gdim, gdim), lambda b, s: (layer, 0, 0, 0)),
            pl.BlockSpec((1, width), lambda b, s: (0, 0)),
        ],
        out_specs=pl.BlockSpec((1, ts, width), lambda b, s: (b, s, 0)),
        out_shape=jax.ShapeDtypeStruct((bsz, seq, width), BF16),
        scratch_shapes=[pltpu.VMEM((d, width), BF16), pltpu.VMEM((POOL_HALO, width), F32)],
        compiler_params=_params(("arbitrary", "arbitrary")),
        name="pool",
    )(xb, w_in, pw, sc)


def _sgu_kernel(x_ref, wu_ref, wv_ref, g_ref, b_ref, ws_ref, bias_ref, o_ref, w_ref):
    @pl.when(pl.program_id(0) == 0)
    def _():
        w_ref[0] = wu_ref[...].astype(BF16)
        w_ref[1] = wv_ref[...].astype(BF16)

    x = x_ref[...]
    t = x.shape[0]
    gu = _gelu_tanh(_dot(x, w_ref[0]))
    vn = _layer_norm(_gelu_tanh(_dot(x, w_ref[1])), g_ref[...], b_ref[...]).astype(BF16)
    n_chunks = t // SG_CHUNK
    gdim = vn.shape[1] // SG_GROUPS
    ri = lax.broadcasted_iota(jnp.int32, (SG_CHUNK, SG_CHUNK), 0)
    ci = lax.broadcasted_iota(jnp.int32, (SG_CHUNK, SG_CHUNK), 1)
    for g in range(SG_GROUPS):
        cs = slice(g * gdim, (g + 1) * gdim)
        w = jnp.where(ri >= ci, ws_ref[g], 0.0).astype(BF16)
        rhs = jnp.concatenate([vn[c * SG_CHUNK:(c + 1) * SG_CHUNK, cs] for c in range(n_chunks)], axis=1)
        mixed = _dot(w, rhs)
        for c in range(n_chunks):
            rs = slice(c * SG_CHUNK, (c + 1) * SG_CHUNK)
            y = gu[rs, cs] * (mixed[:, c * gdim:(c + 1) * gdim] + bias_ref[:, cs])
            o_ref[rs, cs] = y.astype(o_ref.dtype)


def _sgu(layer, xb2, w_in, u_block, v_block, g, b, ws, bias):
    n, d = xb2.shape
    width = g.shape[1]
    ts = min(TOKEN_TILE, n)
    const2 = lambda i: (0, 0)
    return pl.pallas_call(
        _sgu_kernel,
        grid=(n // ts,),
        in_specs=[
            pl.BlockSpec((ts, d), lambda i: (i, 0)),
            pl.BlockSpec((None, d, width), lambda i: (layer, 0, u_block), pipeline_mode=pl.Buffered(1)),
            pl.BlockSpec((None, d, width), lambda i: (layer, 0, v_block), pipeline_mode=pl.Buffered(1)),
            pl.BlockSpec((1, width), const2),
            pl.BlockSpec((1, width), const2),
            pl.BlockSpec((None,) + ws.shape[1:], lambda i: (layer, 0, 0, 0)),
            pl.BlockSpec(bias.shape, const2),
        ],
        out_specs=pl.BlockSpec((ts, width), lambda i: (i, 0)),
        out_shape=jax.ShapeDtypeStruct((n, width), BF16),
        scratch_shapes=[pltpu.VMEM((2, d, width), BF16)],
        compiler_params=_params(("arbitrary",)),
        name="sgu",
    )(xb2, w_in, w_in, g, b, ws, bias)


def _merge_kernel(x_ref, yh_ref, yp_ref, ys_ref, g0_ref, g1_ref, g2_ref, p0_ref, p1_ref, p2_ref, o_ref,
                  gw_ref, pw_ref):
    @pl.when(pl.program_id(1) == 0)
    def _():
        for k, (g_ref, p_ref) in enumerate(((g0_ref, p0_ref), (g1_ref, p1_ref), (g2_ref, p2_ref))):
            gw_ref[k] = g_ref[...].astype(BF16)
            pw_ref[k] = p_ref[...].astype(BF16)

    x = x_ref[...]
    acc = None
    for k, y_ref in enumerate((yh_ref, yp_ref, ys_ref)):
        term = _sigmoid(_dot(x, gw_ref[k])) * _dot(y_ref[...], pw_ref[k])
        acc = term if acc is None else acc + term
    o_ref[...] = acc.astype(o_ref.dtype)


def _merge(layer, xb2, yh, yp, ys, w_in, gate_col0, ph, pp, ps):
    n, d = xb2.shape
    tm = min(TOKEN_TILE, n)
    tn = MERGE_TN
    gblk0 = gate_col0 // tn
    per_branch = d // tn
    width = ph.shape[1]
    assert pp.shape[1] == width and ps.shape[1] == width
    once = pl.Buffered(1)
    yspec = lambda y: pl.BlockSpec((tm, y.shape[1]), lambda j, i: (i, 0))
    gspec = lambda k: pl.BlockSpec((None, d, tn), lambda j, i, k=k: (layer, 0, gblk0 + k * per_branch + j),
                                   pipeline_mode=once)
    pspec = lambda: pl.BlockSpec((None, width, tn), lambda j, i: (layer, 0, j), pipeline_mode=once)
    return pl.pallas_call(
        _merge_kernel,
        grid=(d // tn, n // tm),
        in_specs=[pl.BlockSpec((tm, d), lambda j, i: (i, 0)), yspec(yh), yspec(yp), yspec(ys),
                  gspec(0), gspec(1), gspec(2), pspec(), pspec(), pspec()],
        out_specs=pl.BlockSpec((tm, tn), lambda j, i: (i, j)),
        out_shape=jax.ShapeDtypeStruct((n, d), BF16),
        scratch_shapes=[pltpu.VMEM((N_BRANCH, d, tn), BF16), pltpu.VMEM((N_BRANCH, width, tn), BF16)],
        compiler_params=_params(("arbitrary", "arbitrary")),
        name="merge",
    )(xb2, yh, yp, ys, w_in, w_in, w_in, ph, pp, ps)


def _outln_kernel(alpha, m_ref, w_ref, x_ref, g_ref, b_ref, o_ref, ob_ref, wb_ref, mix_ref):
    @pl.when(pl.program_id(0) == 0)
    def _():
        wb_ref[...] = w_ref[...].astype(BF16)

    mix_ref[...] = _dot(m_ref[...], wb_ref[...])
    _residual_layer_norm(alpha, x_ref, mix_ref, g_ref[...], b_ref[...], o_ref, ob_ref)


def _outln(alpha, layer, merged, w_out, x2, g, b):
    n, d = x2.shape
    tm = min(TOKEN_TILE // 2, n)
    const2 = lambda i: (0, 0)
    row = pl.BlockSpec((tm, d), lambda i: (i, 0))
    return pl.pallas_call(
        functools.partial(_outln_kernel, alpha),
        grid=(n // tm,),
        in_specs=[row, pl.BlockSpec((None, d, d), lambda i: (layer, 0, 0), pipeline_mode=pl.Buffered(1)), row,
                  pl.BlockSpec((1, d), const2), pl.BlockSpec((1, d), const2)],
        out_specs=[row, row],
        out_shape=[jax.ShapeDtypeStruct((n, d), F32), jax.ShapeDtypeStruct((n, d), BF16)],
        scratch_shapes=[pltpu.VMEM((d, d), BF16), pltpu.VMEM((tm, d), F32)],
        compiler_params=_params(("arbitrary",)),
        name="outln",
    )(merged, w_out, x2, g, b)


def _causal_conv(h, halo, cw, cb, row):
    n = halo.shape[0]
    h1 = jnp.where(row == 0, halo[n - 1:n, :], pltpu.roll(h, 1, axis=0))
    h2 = jnp.where(row == 0, halo[n - 2:n - 1, :], jnp.where(row == 1, halo[n - 1:n, :], pltpu.roll(h, 2, axis=0)))
    return cb + cw[0:1, :] * h2 + cw[1:2, :] * h1 + cw[2:3, :] * h


def _ffn_kernel(alpha, tiles_per_seq, xb_ref, xh_ref, wa_ref, wb_ref, cwa_ref, cwb_ref, cba_ref, cbb_ref,
                wd_ref, x_ref, g_ref, b_ref, o_ref, ob_ref, acc_ref):
    i = pl.program_id(0)
    j = pl.program_id(1)

    @pl.when(j == 0)
    def _():
        acc_ref[...] = jnp.zeros_like(acc_ref)

    xb = xb_ref[...]
    xh = jnp.where(i % tiles_per_seq == 0, jnp.zeros_like(xh_ref), xh_ref[...])
    row = lax.broadcasted_iota(jnp.int32, (xb.shape[0], 1), 0)
    gates = []
    for c0 in range(0, wa_ref.shape[1], FFN_COLS):
        cs = pl.ds(c0, FFN_COLS)
        wa = wa_ref[:, cs]
        wb = wb_ref[:, cs]
        a = _causal_conv(_dot(xb, wa), _dot(xh, wa), cwa_ref[:, cs], cba_ref[:, cs], row)
        b = _causal_conv(_dot(xb, wb), _dot(xh, wb), cwb_ref[:, cs], cbb_ref[:, cs], row)
        gates.append((_silu(a) * b).astype(BF16))
    acc_ref[...] += _dot(jnp.concatenate(gates, axis=1), wd_ref[...])

    @pl.when(j == pl.num_programs(1) - 1)
    def _():
        _residual_layer_norm(alpha, x_ref, acc_ref, g_ref[...], b_ref[...], o_ref, ob_ref)


def _ffn(alpha, seq, xb2, x2, w_up, conv_w, conv_b, w_down, g, b):
    n, d = x2.shape
    f = w_down.shape[0]
    tm = min(FFN_TM, seq)
    tf = FFN_TF
    nf = f // tf
    halo_blocks = tm // BF16_ROWS
    row = pl.BlockSpec((tm, d), lambda i, j: (i, 0))
    const2 = lambda i, j: (0, 0)
    return pl.pallas_call(
        functools.partial(_ffn_kernel, alpha, seq // tm),
        grid=(n // tm, nf),
        in_specs=[
            row,
            pl.BlockSpec((BF16_ROWS, d), lambda i, j: (jnp.maximum(i * halo_blocks - 1, 0), 0)),
            pl.BlockSpec((d, tf), lambda i, j: (0, j)),
            pl.BlockSpec((d, tf), lambda i, j: (0, nf + j)),
            pl.BlockSpec((CONV_W, tf), lambda i, j: (0, j)),
            pl.BlockSpec((CONV_W, tf), lambda i, j: (0, nf + j)),
            pl.BlockSpec((1, tf), lambda i, j: (0, j)),
            pl.BlockSpec((1, tf), lambda i, j: (0, nf + j)),
            pl.BlockSpec((tf, d), lambda i, j: (j, 0)),
            row,
            pl.BlockSpec((1, d), const2),
            pl.BlockSpec((1, d), const2),
        ],
        out_specs=[row, row],
        out_shape=[jax.ShapeDtypeStruct((n, d), F32), jax.ShapeDtypeStruct((n, d), BF16)],
        scratch_shapes=[pltpu.VMEM((tm, d), F32)],
        compiler_params=_params(("parallel", "arbitrary")),
        name="ffn",
    )(xb2, xb2, w_up, w_up, conv_w, conv_w, conv_b, conv_b, w_down, x2, g, b)


def kernel(x, w_in, hg_lower_bounds, hg_norm_g, pool_w, pool_scale, sg_ln_g, sg_ln_b, sg_w, sg_b, w_hg_proj, w_pool_proj, w_sg_proj, w_out, ln1_g, ln1_b, w_up, conv_w, conv_b, w_down, ln2_g, ln2_b):
    bsz, seq, d = x.shape
    depth = w_in.shape[0]
    n = bsz * seq
    alpha = (2 * depth) ** 0.25
    hg_width = HG_HEADS * HG_DIM
    pool_width = pool_w.shape[1] * pool_w.shape[2]
    sg_width = sg_ln_g.shape[1]
    pool_col = 4 * hg_width
    u_col = pool_col + pool_width
    v_col = u_col + sg_width
    gate_col = v_col + sg_width
    assert pool_col % pool_width == 0 and u_col % sg_width == 0 and gate_col % MERGE_TN == 0

    x2 = x.reshape(n, d)
    xb2 = x2.astype(BF16)
    lbp = hg_lower_bounds.astype(F32)
    for l in range(depth):
        xb3 = xb2.reshape(bsz, seq, d)
        y_hg, w_up_b, w_down_b = _hgrn(l, xb3, w_in, lbp, hg_norm_g[l].reshape(1, hg_width), w_up, w_down)
        y_hg = y_hg.reshape(n, hg_width)
        y_pool = _pool(l, xb3, w_in, pool_col // pool_width, pool_w,
                       pool_scale[l].reshape(1, pool_width)).reshape(n, pool_width)
        gdim = sg_width // SG_GROUPS
        bias = jnp.repeat(sg_b[l].T, gdim, axis=1)
        y_sg = _sgu(l, xb2, w_in, u_col // sg_width, v_col // sg_width, sg_ln_g[l].reshape(1, sg_width),
                    sg_ln_b[l].reshape(1, sg_width), sg_w, bias)
        merged = _merge(l, xb2, y_hg, y_pool, y_sg, w_in, gate_col, w_hg_proj, w_pool_proj, w_sg_proj)
        x2, xb2 = _outln(alpha, l, merged, w_out, x2, ln1_g[l].reshape(1, d), ln1_b[l].reshape(1, d))
        x2, xb2 = _ffn(alpha, seq, xb2, x2, w_up_b, conv_w[l], conv_b[l].reshape(1, -1), w_down_b,
                       ln2_g[l].reshape(1, d), ln2_b[l].reshape(1, d))
    return x2.reshape(bsz, seq, d)
```

```python
import functools
import math

import jax
import jax.numpy as jnp
from jax import lax
from jax.experimental import pallas as pl
from jax.experimental.pallas import tpu as pltpu

F32 = jnp.float32
BF16 = jnp.bfloat16

HG_HEADS = 8
HG_DIM = 128
POOL_WINDOWS = (2, 4, 8, 16)
SG_GROUPS = 8
SG_CHUNK = 128
N_BRANCH = 3
CONV_W = 3
LN_EPS = 1e-5
RMS_EPS = 1e-6

LANES = 128
SUBLANES = 8
BF16_ROWS = 16
VMEM_LIMIT_BYTES = 60 * 1024 * 1024

HG_PAIR = 2 * HG_DIM
HG_CHUNK = 128
HG_LEVELS = tuple(HG_CHUNK >> (i + 1) for i in range(int(math.log2(HG_CHUNK))))
POOL_HALO = 16
TOKEN_TILE = 512
MERGE_TN = 512
FFN_TM = 512
FFN_TF = 512
FFN_COLS = 256
LN_ROWS = 128


def _params(semantics):
    return pltpu.CompilerParams(dimension_semantics=semantics, vmem_limit_bytes=VMEM_LIMIT_BYTES)


def _dot(a, b):
    return jnp.dot(a, b, preferred_element_type=F32)


def _dot_nt(a, b):
    return lax.dot_general(a, b, (((1,), (1,)), ((), ())), preferred_element_type=F32)


def _dot_tn(a, b):
    return lax.dot_general(a, b, (((0,), (0,)), ((), ())), preferred_element_type=F32)


def _sigmoid(x):
    return 1.0 / (1.0 + jnp.exp(-x))


def _silu(x):
    return x * _sigmoid(x)


def _gelu_tanh(x):
    c = math.sqrt(2.0 / math.pi)
    return 0.5 * x * (1.0 + jnp.tanh(c * (x + 0.044715 * (x * x * x))))


def _layer_norm(y, g, b):
    mu = jnp.mean(y, axis=-1, keepdims=True)
    d = y - mu
    var = jnp.mean(d * d, axis=-1, keepdims=True)
    return d * lax.rsqrt(var + LN_EPS) * g + b


def _residual_layer_norm(alpha, x_ref, y_ref, g, b, o_ref, ob_ref):
    def body(i, carry):
        rs = pl.ds(pl.multiple_of(i * LN_ROWS, LN_ROWS), LN_ROWS)
        out = _layer_norm(alpha * x_ref[rs, :] + y_ref[rs, :], g, b)
        o_ref[rs, :] = out
        ob_ref[rs, :] = out.astype(ob_ref.dtype)
        return carry
    lax.fori_loop(0, x_ref.shape[0] // LN_ROWS, body, 0)


def _block_reference(b, h, row):
    t, c = b.shape
    if 2 * h >= SUBLANES:
        blk = b.reshape(t // (2 * h), 2 * h, c)
        ref = jnp.broadcast_to(blk[:, h - 1:h, :], blk.shape)
        return ref.reshape(t, c)
    pos = row & (2 * h - 1)
    out = b
    for p in range(2 * h):
        off = p - (h - 1)
        if off == 0:
            continue
        shifted = pltpu.roll(b, off % t, axis=0)
        out = jnp.where(pos == p, shifted, out)
    return out


def _hgrn_kernel(layer, x_ref, wq_ref, wf_ref, wi_ref, wg_ref, lbp_ref, ng_ref, up_ref, down_ref,
                 o_ref, upb_ref, downb_ref, w_ref, state_ref):
    upb_ref[...] = up_ref[...].astype(BF16)
    downb_ref[...] = down_ref[...].astype(BF16)

    @pl.when(pl.program_id(2) == 0)
    def _():
        state_ref[...] = jnp.zeros_like(state_ref)
        for k, src in enumerate((wq_ref, wf_ref, wi_ref, wg_ref)):
            w_ref[k] = src[...].astype(BF16)

    x = x_ref[0].astype(BF16)
    t = x.shape[0]
    zq = _dot(x, w_ref[0])
    zf = _dot(x, w_ref[1])
    zi = _dot(x, w_ref[2])
    zg = _dot(x, w_ref[3])

    qf = _silu(zq)
    e = jnp.exp(-jnp.abs(zf))
    inv1pe = 1.0 / (1.0 + e)
    log_sig = jnp.minimum(zf, 0.0) - jnp.log(1.0 + e)
    sig_neg = jnp.where(zf >= 0.0, e * inv1pe, inv1pe)
    if layer == 0:
        lf = log_sig
        kf = sig_neg
    else:
        p = lbp_ref[...]
        rows = [p[i:i + 1, :] for i in range(p.shape[0])]
        m = functools.reduce(jnp.maximum, rows)
        ex = [jnp.exp(r - m) for r in rows]
        lb = functools.reduce(jnp.add, ex[1:layer + 1]) / functools.reduce(jnp.add, ex)
        la = jnp.log(lb)
        lbv = jnp.log(1.0 - lb) + log_sig
        lf = jnp.maximum(la, lbv) + jnp.log(1.0 + jnp.exp(-jnp.abs(la - lbv)))
        kf = (1.0 - lb) * sig_neg

    row = lax.broadcasted_iota(jnp.int32, (t, 1), 0)
    pos = row & (HG_CHUNK - 1)
    b = lf
    sh = 1
    while sh < HG_CHUNK:
        b = b + jnp.where(pos >= sh, pltpu.roll(b, sh, axis=0), 0.0)
        sh *= 2

    n_chunks = t // HG_CHUNK
    n_heads = HG_PAIR // HG_DIM
    ti = lax.broadcasted_iota(jnp.int32, (HG_CHUNK, HG_CHUNK), 0)
    si = lax.broadcasted_iota(jnp.int32, (HG_CHUNK, HG_CHUNK), 1)

    scores = [[None] * n_heads for _ in range(n_chunks)]
    for h in HG_LEVELS:
        ref = _block_reference(b, h, row)
        w = jnp.exp(-jnp.abs(b - ref))
        second = (row & h) != 0
        qt = jnp.where(second, qf * w, 0.0).astype(BF16)
        kt = jnp.where(second, 0.0, kf * w).astype(BF16)
        same = (ti // (2 * h)) == (si // (2 * h))
        for c in range(n_chunks):
            r0 = c * HG_CHUNK
            for hh in range(n_heads):
                c0 = hh * HG_DIM
                a = _dot_nt(qt[r0:r0 + HG_CHUNK, c0:c0 + HG_DIM], kt[r0:r0 + HG_CHUNK, c0:c0 + HG_DIM])
                a = jnp.where(same, a, 0.0)
                scores[c][hh] = a if scores[c][hh] is None else scores[c][hh] + a

    vb = zi.astype(BF16)
    qk = qf * kf
    qd = (qf * jnp.exp(b)).astype(BF16)
    ng = ng_ref[...]
    gate = _silu(zg)
    for hh in range(n_heads):
        c0 = hh * HG_DIM
        st = state_ref[hh]
        for c in range(n_chunks):
            r0 = c * HG_CHUNK
            rs = slice(r0, r0 + HG_CHUNK)
            cs = slice(c0, c0 + HG_DIM)
            b_c = b[rs, cs]
            b_last = b_c[HG_CHUNK - 1:HG_CHUNK, :]
            v_c = vb[rs, cs]
            o = _dot(scores[c][hh].astype(BF16), v_c)
            o = o + jnp.sum(qk[rs, cs], axis=-1, keepdims=True) * zi[rs, cs]
            o = o + _dot_nt(qd[rs, cs], st.astype(BF16))
            kd = (kf[rs, cs] * jnp.exp(b_last - b_c)).astype(BF16)
            st = jnp.exp(b_last) * st + _dot_tn(v_c, kd)
            o = o * lax.rsqrt(jnp.mean(o * o, axis=-1, keepdims=True) + RMS_EPS)
            o = o * ng[:, cs] * gate[rs, cs]
            o_ref[0, rs, cs] = o.astype(o_ref.dtype)
        state_ref[hh] = st


def _hgrn(layer, xb, w_in, lbp, ng, w_up, w_down):
    bsz, seq, d = xb.shape
    width = HG_HEADS * HG_DIM
    n_pairs = width // HG_PAIR
    ts = min(TOKEN_TILE, seq)
    n_s = seq // ts
    n_steps = bsz * n_pairs * n_s

    def slab_rows(w):
        rows = w.shape[1]
        r = next(r for r in range(BF16_ROWS, rows + 1, BF16_ROWS) if rows % r == 0 and rows // r <= n_steps)
        return r, rows // r

    step = lambda b, p, s: (b * n_pairs + p) * n_s + s

    def slab_specs(w):
        r, n_slabs = slab_rows(w)
        idx = lambda b, p, s: jnp.minimum(step(b, p, s), n_slabs - 1)
        return (pl.BlockSpec((None, r, w.shape[2]), lambda b, p, s: (layer, idx(b, p, s), 0)),
                pl.BlockSpec((r, w.shape[2]), lambda b, p, s: (idx(b, p, s), 0)))

    (up_in, up_out), (down_in, down_out) = slab_specs(w_up), slab_specs(w_down)
    wspec = lambda k: pl.BlockSpec((None, d, HG_PAIR), lambda b, p, s, k=k: (layer, 0, k * n_pairs + p))
    return pl.pallas_call(
        functools.partial(_hgrn_kernel, layer),
        grid=(bsz, n_pairs, n_s),
        in_specs=[
            pl.BlockSpec((1, ts, d), lambda b, p, s: (b, s, 0)),
            wspec(0), wspec(1), wspec(2), wspec(3),
            pl.BlockSpec((lbp.shape[0], HG_PAIR), lambda b, p, s: (0, p)),
            pl.BlockSpec((1, HG_PAIR), lambda b, p, s: (0, p)),
            up_in, down_in,
        ],
        out_specs=[pl.BlockSpec((1, ts, HG_PAIR), lambda b, p, s: (b, s, p)), up_out, down_out],
        out_shape=[jax.ShapeDtypeStruct((bsz, seq, width), BF16),
                   jax.ShapeDtypeStruct(w_up.shape[1:], BF16), jax.ShapeDtypeStruct(w_down.shape[1:], BF16)],
        scratch_shapes=[pltpu.VMEM((4, d, HG_PAIR), BF16), pltpu.VMEM((HG_PAIR // HG_DIM, HG_DIM, HG_DIM), F32)],
        compiler_params=_params(("arbitrary", "arbitrary", "arbitrary")),
        name="hgrn",
    )(xb, w_in, w_in, w_in, w_in, lbp, ng, w_up, w_down)


def _pool_kernel(x_ref, wp_ref, pw_ref, sc_ref, o_ref, w_ref, halo_ref):
    s = pl.program_id(1)

    @pl.when(jnp.logical_and(pl.program_id(0) == 0, s == 0))
    def _():
        w_ref[...] = wp_ref[...].astype(BF16)

    @pl.when(s == 0)
    def _():
        halo_ref[...] = jnp.zeros_like(halo_ref)

    x = x_ref[0].astype(BF16)
    t = x.shape[0]
    p = _dot(x, w_ref[...])
    ext = jnp.concatenate([halo_ref[...], p], axis=0)
    halo_ref[...] = p[t - POOL_HALO:, :]
    avail = s * t + lax.broadcasted_iota(jnp.int32, (t, 1), 0) + 1
    gdim = pw_ref.shape[1]
    for g, w in enumerate(POOL_WINDOWS):
        cs = slice(g * gdim, (g + 1) * gdim)
        acc = ext[:, cs]
        sh = 1
        while sh < w:
            acc = acc + pltpu.roll(acc, sh, axis=0)
            sh *= 2
        cnt = jnp.minimum(avail, w).astype(F32)
        pooled = acc[POOL_HALO:, :] * (1.0 / cnt) - p[:, cs]
        y = _dot(pooled.astype(BF16), pw_ref[g].astype(BF16)) * sc_ref[:, cs]
        o_ref[0, :, cs] = y.astype(o_ref.dtype)


def _pool(layer, xb, w_in, col_block, pw, sc):
    bsz, seq, d = xb.shape
    _, g, gdim, _ = pw.shape
    width = g * gdim
    ts = min(TOKEN_TILE, seq)
    return pl.pallas_call(
        _pool_kernel,
        grid=(bsz, seq // ts),
        in_specs=[
            pl.BlockSpec((1, ts, d), lambda b, s: (b, s, 0)),
            pl.BlockSpec((None, d, width), lambda b, s: (layer, 0, col_block), pipeline_mode=pl.Buffered(1)),
            pl.BlockSpec((None, g, gdim, gdim), lambda b, s: (layer, 0, 0, 0)),
            pl.BlockSpec((1, width), lambda b, s: (0, 0)),
        ],
        out_specs=pl.BlockSpec((1, ts, width), lambda b, s: (b, s, 0)),
        out_shape=jax.ShapeDtypeStruct((bsz, seq, width), BF16),
        scratch_shapes=[pltpu.VMEM((d, width), BF16), pltpu.VMEM((POOL_HALO, width), F32)],
        compiler_params=_params(("arbitrary", "arbitrary")),
        name="pool",
    )(xb, w_in, pw, sc)


def _sgu_kernel(x_ref, wu_ref, wv_ref, g_ref, b_ref, ws_ref, bias_ref, o_ref, w_ref):
    @pl.when(pl.program_id(0) == 0)
    def _():
        w_ref[0] = wu_ref[...].astype(BF16)
        w_ref[1] = wv_ref[...].astype(BF16)

    x = x_ref[...].astype(BF16)
    t = x.shape[0]
    gu = _gelu_tanh(_dot(x, w_ref[0]))
    vn = _layer_norm(_gelu_tanh(_dot(x, w_ref[1])), g_ref[...], b_ref[...]).astype(BF16)
    n_chunks = t // SG_CHUNK
    gdim = vn.shape[1] // SG_GROUPS
    ri = lax.broadcasted_iota(jnp.int32, (SG_CHUNK, SG_CHUNK), 0)
    ci = lax.broadcasted_iota(jnp.int32, (SG_CHUNK, SG_CHUNK), 1)
    for g in range(SG_GROUPS):
        cs = slice(g * gdim, (g + 1) * gdim)
        w = jnp.where(ri >= ci, ws_ref[g], 0.0).astype(BF16)
        rhs = jnp.concatenate([vn[c * SG_CHUNK:(c + 1) * SG_CHUNK, cs] for c in range(n_chunks)], axis=1)
        mixed = _dot(w, rhs)
        for c in range(n_chunks):
            rs = slice(c * SG_CHUNK, (c + 1) * SG_CHUNK)
            y = gu[rs, cs] * (mixed[:, c * gdim:(c + 1) * gdim] + bias_ref[:, cs])
            o_ref[rs, cs] = y.astype(o_ref.dtype)


def _sgu(layer, xb2, w_in, u_block, v_block, g, b, ws, bias):
    n, d = xb2.shape
    width = g.shape[1]
    ts = min(TOKEN_TILE, n)
    const2 = lambda i: (0, 0)
    return pl.pallas_call(
        _sgu_kernel,
        grid=(n // ts,),
        in_specs=[
            pl.BlockSpec((ts, d), lambda i: (i, 0)),
            pl.BlockSpec((None, d, width), lambda i: (layer, 0, u_block), pipeline_mode=pl.Buffered(1)),
            pl.BlockSpec((None, d, width), lambda i: (layer, 0, v_block), pipeline_mode=pl.Buffered(1)),
            pl.BlockSpec((1, width), const2),
            pl.BlockSpec((1, width), const2),
            pl.BlockSpec((None,) + ws.shape[1:], lambda i: (layer, 0, 0, 0)),
            pl.BlockSpec(bias.shape, const2),
        ],
        out_specs=pl.BlockSpec((ts, width), lambda i: (i, 0)),
        out_shape=jax.ShapeDtypeStruct((n, width), BF16),
        scratch_shapes=[pltpu.VMEM((2, d, width), BF16)],
        compiler_params=_params(("arbitrary",)),
        name="sgu",
    )(xb2, w_in, w_in, g, b, ws, bias)


def _merge_kernel(x_ref, yh_ref, yp_ref, ys_ref, g0_ref, g1_ref, g2_ref, p0_ref, p1_ref, p2_ref, o_ref,
                  gw_ref, pw_ref):
    @pl.when(pl.program_id(1) == 0)
    def _():
        for k, (g_ref, p_ref) in enumerate(((g0_ref, p0_ref), (g1_ref, p1_ref), (g2_ref, p2_ref))):
            gw_ref[k] = g_ref[...].astype(BF16)
            pw_ref[k] = p_ref[...].astype(BF16)

    x = x_ref[...].astype(BF16)
    acc = None
    for k, y_ref in enumerate((yh_ref, yp_ref, ys_ref)):
        term = _sigmoid(_dot(x, gw_ref[k])) * _dot(y_ref[...], pw_ref[k])
        acc = term if acc is None else acc + term
    o_ref[...] = acc.astype(o_ref.dtype)


def _merge(layer, xb2, yh, yp, ys, w_in, gate_col0, ph, pp, ps):
    n, d = xb2.shape
    tm = min(TOKEN_TILE, n)
    tn = MERGE_TN
    gblk0 = gate_col0 // tn
    per_branch = d // tn
    width = ph.shape[1]
    assert pp.shape[1] == width and ps.shape[1] == width
    yspec = lambda y: pl.BlockSpec((tm, y.shape[1]), lambda j, i: (i, 0))
    gspec = lambda k: pl.BlockSpec((None, d, tn), lambda j, i, k=k: (layer, 0, gblk0 + k * per_branch + j))
    pspec = lambda: pl.BlockSpec((None, width, tn), lambda j, i: (layer, 0, j), pipeline_mode=pl.Buffered(1))
    return pl.pallas_call(
        _merge_kernel,
        grid=(d // tn, n // tm),
        in_specs=[pl.BlockSpec((tm, d), lambda j, i: (i, 0)), yspec(yh), yspec(yp), yspec(ys),
                  gspec(0), gspec(1), gspec(2), pspec(), pspec(), pspec()],
        out_specs=pl.BlockSpec((tm, tn), lambda j, i: (i, j)),
        out_shape=jax.ShapeDtypeStruct((n, d), BF16),
        scratch_shapes=[pltpu.VMEM((N_BRANCH, d, tn), BF16), pltpu.VMEM((N_BRANCH, width, tn), BF16)],
        compiler_params=_params(("arbitrary", "arbitrary")),
        name="merge",
    )(xb2, yh, yp, ys, w_in, w_in, w_in, ph, pp, ps)


def _outln_kernel(alpha, m_ref, w_ref, x_ref, g_ref, b_ref, o_ref, ob_ref, wb_ref, mix_ref):
    @pl.when(pl.program_id(0) == 0)
    def _():
        wb_ref[...] = w_ref[...].astype(BF16)

    mix_ref[...] = _dot(m_ref[...], wb_ref[...])
    _residual_layer_norm(alpha, x_ref, mix_ref, g_ref[...], b_ref[...], o_ref, ob_ref)


def _outln(alpha, layer, merged, w_out, x2, g, b):
    n, d = x2.shape
    tm = min(TOKEN_TILE // 2, n)
    const2 = lambda i: (0, 0)
    row = pl.BlockSpec((tm, d), lambda i: (i, 0))
    return pl.pallas_call(
        functools.partial(_outln_kernel, alpha),
        grid=(n // tm,),
        in_specs=[row, pl.BlockSpec((None, d, d), lambda i: (layer, 0, 0), pipeline_mode=pl.Buffered(1)), row,
                  pl.BlockSpec((1, d), const2), pl.BlockSpec((1, d), const2)],
        out_specs=[row, row],
        out_shape=[jax.ShapeDtypeStruct((n, d), F32), jax.ShapeDtypeStruct((n, d), BF16)],
        scratch_shapes=[pltpu.VMEM((d, d), BF16), pltpu.VMEM((tm, d), F32)],
        compiler_params=_params(("arbitrary",)),
        name="outln",
    )(merged, w_out, x2, g, b)


def _causal_conv(h, halo, cw, cb, row):
    n = halo.shape[0]
    h1 = jnp.where(row == 0, halo[n - 1:n, :], pltpu.roll(h, 1, axis=0))
    h2 = jnp.where(row == 0, halo[n - 2:n - 1, :], jnp.where(row == 1, halo[n - 1:n, :], pltpu.roll(h, 2, axis=0)))
    return cb + cw[0:1, :] * h2 + cw[1:2, :] * h1 + cw[2:3, :] * h


def _ffn_kernel(alpha, tiles_per_seq, xb_ref, xh_ref, wa_ref, wb_ref, cwa_ref, cwb_ref, cba_ref, cbb_ref,
                wd_ref, x_ref, g_ref, b_ref, o_ref, ob_ref, acc_ref):
    i = pl.program_id(0)
    j = pl.program_id(1)

    @pl.when(j == 0)
    def _():
        acc_ref[...] = jnp.zeros_like(acc_ref)

    xb = xb_ref[...]
    xh = jnp.where(i % tiles_per_seq == 0, jnp.zeros_like(xh_ref), xh_ref[...])
    row = lax.broadcasted_iota(jnp.int32, (xb.shape[0], 1), 0)
    gates = []
    for c0 in range(0, wa_ref.shape[1], FFN_COLS):
        cs = pl.ds(c0, FFN_COLS)
        wa = wa_ref[:, cs]
        wb = wb_ref[:, cs]
        a = _causal_conv(_dot(xb, wa), _dot(xh, wa), cwa_ref[:, cs], cba_ref[:, cs], row)
        b = _causal_conv(_dot(xb, wb), _dot(xh, wb), cwb_ref[:, cs], cbb_ref[:, cs], row)
        gates.append((_silu(a) * b).astype(BF16))
    acc_ref[...] += _dot(jnp.concatenate(gates, axis=1), wd_ref[...])

    @pl.when(j == pl.num_programs(1) - 1)
    def _():
        _residual_layer_norm(alpha, x_ref, acc_ref, g_ref[...], b_ref[...], o_ref, ob_ref)


def _ffn(alpha, seq, xb2, x2, w_up, conv_w, conv_b, w_down, g, b):
    n, d = x2.shape
    f = w_down.shape[0]
    tm = min(FFN_TM, seq)
    tf = FFN_TF
    nf = f // tf
    halo_blocks = tm // BF16_ROWS
    row = pl.BlockSpec((tm, d), lambda i, j: (i, 0))
    const2 = lambda i, j: (0, 0)
    return pl.pallas_call(
        functools.partial(_ffn_kernel, alpha, seq // tm),
        grid=(n // tm, nf),
        in_specs=[
            row,
            pl.BlockSpec((BF16_ROWS, d), lambda i, j: (jnp.maximum(i * halo_blocks - 1, 0), 0)),
            pl.BlockSpec((d, tf), lambda i, j: (0, j)),
            pl.BlockSpec((d, tf), lambda i, j: (0, nf + j)),
            pl.BlockSpec((CONV_W, tf), lambda i, j: (0, j)),
            pl.BlockSpec((CONV_W, tf), lambda i, j: (0, nf + j)),
            pl.BlockSpec((1, tf), lambda i, j: (0, j)),
            pl.BlockSpec((1, tf), lambda i, j: (0, nf + j)),
            pl.BlockSpec((tf, d), lambda i, j: (j, 0)),
            row,
            pl.BlockSpec((1, d), const2),
            pl.BlockSpec((1, d), const2),
        ],
        out_specs=[row, row],
        out_shape=[jax.ShapeDtypeStruct((n, d), F32), jax.ShapeDtypeStruct((n, d), BF16)],
        scratch_shapes=[pltpu.VMEM((tm, d), F32)],
        compiler_params=_params(("parallel", "arbitrary")),
        name="ffn",
    )(xb2, xb2, w_up, w_up, conv_w, conv_w, conv_b, conv_b, w_down, x2, g, b)


def kernel(x, w_in, hg_lower_bounds, hg_norm_g, pool_w, pool_scale, sg_ln_g, sg_ln_b, sg_w, sg_b, w_hg_proj, w_pool_proj, w_sg_proj, w_out, ln1_g, ln1_b, w_up, conv_w, conv_b, w_down, ln2_g, ln2_b):
    bsz, seq, d = x.shape
    depth = w_in.shape[0]
    n = bsz * seq
    alpha = (2 * depth) ** 0.25
    hg_width = HG_HEADS * HG_DIM
    pool_width = pool_w.shape[1] * pool_w.shape[2]
    sg_width = sg_ln_g.shape[1]
    pool_col = 4 * hg_width
    u_col = pool_col + pool_width
    v_col = u_col + sg_width
    gate_col = v_col + sg_width
    assert pool_col % pool_width == 0 and u_col % sg_width == 0 and gate_col % MERGE_TN == 0

    x2 = x.reshape(n, d)
    xb2 = x2
    lbp = hg_lower_bounds.astype(F32)
    for l in range(depth):
        xb3 = xb2.reshape(bsz, seq, d)
        y_hg, w_up_b, w_down_b = _hgrn(l, xb3, w_in, lbp, hg_norm_g[l].reshape(1, hg_width), w_up, w_down)
        y_hg = y_hg.reshape(n, hg_width)
        y_pool = _pool(l, xb3, w_in, pool_col // pool_width, pool_w,
                       pool_scale[l].reshape(1, pool_width)).reshape(n, pool_width)
        gdim = sg_width // SG_GROUPS
        bias = jnp.repeat(sg_b[l].T, gdim, axis=1)
        y_sg = _sgu(l, xb2, w_in, u_col // sg_width, v_col // sg_width, sg_ln_g[l].reshape(1, sg_width),
                    sg_ln_b[l].reshape(1, sg_width), sg_w, bias)
        merged = _merge(l, xb2, y_hg, y_pool, y_sg, w_in, gate_col, w_hg_proj, w_pool_proj, w_sg_proj)
        x2, xb2 = _outln(alpha, l, merged, w_out, x2, ln1_g[l].reshape(1, d), ln1_b[l].reshape(1, d))
        x2, xb2 = _ffn(alpha, seq, xb2, x2, w_up_b, conv_w[l], conv_b[l].reshape(1, -1), w_down_b,
                       ln2_g[l].reshape(1, d), ln2_b[l].reshape(1, d))
    return x2.reshape(bsz, seq, d)
```

```python
import functools
import math

import jax
import jax.numpy as jnp
from jax import lax
from jax.experimental import pallas as pl
from jax.experimental.pallas import tpu as pltpu

F32 = jnp.float32
BF16 = jnp.bfloat16

HG_HEADS = 8
HG_DIM = 128
POOL_WINDOWS = (2, 4, 8, 16)
SG_GROUPS = 8
SG_CHUNK = 128
N_BRANCH = 3
CONV_W = 3
LN_EPS = 1e-5
RMS_EPS = 1e-6

LANES = 128
SUBLANES = 8
BF16_ROWS = 16
VMEM_LIMIT_BYTES = 60 * 1024 * 1024

HG_PAIR = 2 * HG_DIM
HG_CHUNK = 128
HG_LEVELS = tuple(HG_CHUNK >> (i + 1) for i in range(int(math.log2(HG_CHUNK))))
POOL_HALO = 16
TOKEN_TILE = 512
MERGE_TN = 512
FFN_TM = 512
FFN_TF = 512
FFN_COLS = 256
LN_ROWS = 128


def _params(semantics):
    return pltpu.CompilerParams(dimension_semantics=semantics, vmem_limit_bytes=VMEM_LIMIT_BYTES)


def _dot(a, b):
    return jnp.dot(a, b, preferred_element_type=F32)


def _dot_nt(a, b):
    return lax.dot_general(a, b, (((1,), (1,)), ((), ())), preferred_element_type=F32)


def _dot_tn(a, b):
    return lax.dot_general(a, b, (((0,), (0,)), ((), ())), preferred_element_type=F32)


def _sigmoid(x):
    return 1.0 / (1.0 + jnp.exp(-x))


def _silu(x):
    return x * _sigmoid(x)


def _gelu_tanh(x):
    c = math.sqrt(2.0 / math.pi)
    return 0.5 * x * (1.0 + jnp.tanh(c * (x + 0.044715 * (x * x * x))))


def _layer_norm(y, g, b):
    mu = jnp.mean(y, axis=-1, keepdims=True)
    d = y - mu
    var = jnp.mean(d * d, axis=-1, keepdims=True)
    return d * lax.rsqrt(var + LN_EPS) * g + b


def _residual_layer_norm(alpha, x_ref, y_ref, g, b, o_ref, ob_ref):
    def body(i, carry):
        rs = pl.ds(pl.multiple_of(i * LN_ROWS, LN_ROWS), LN_ROWS)
        out = _layer_norm(alpha * x_ref[rs, :] + y_ref[rs, :], g, b)
        o_ref[rs, :] = out
        ob_ref[rs, :] = out.astype(ob_ref.dtype)
        return carry
    lax.fori_loop(0, x_ref.shape[0] // LN_ROWS, body, 0)


def _block_reference(b, h, row):
    t, c = b.shape
    if 2 * h >= SUBLANES:
        blk = b.reshape(t // (2 * h), 2 * h, c)
        ref = jnp.broadcast_to(blk[:, h - 1:h, :], blk.shape)
        return ref.reshape(t, c)
    pos = row & (2 * h - 1)
    out = b
    for p in range(2 * h):
        off = p - (h - 1)
        if off == 0:
            continue
        shifted = pltpu.roll(b, off % t, axis=0)
        out = jnp.where(pos == p, shifted, out)
    return out


def _hgrn_kernel(layer, x_ref, wq_ref, wf_ref, wi_ref, wg_ref, lbp_ref, ng_ref, up_ref, down_ref,
                 o_ref, upb_ref, downb_ref, w_ref, state_ref):
    upb_ref[...] = up_ref[...].astype(BF16)
    downb_ref[...] = down_ref[...].astype(BF16)

    @pl.when(pl.program_id(2) == 0)
    def _():
        state_ref[...] = jnp.zeros_like(state_ref)
        for k, src in enumerate((wq_ref, wf_ref, wi_ref, wg_ref)):
            w_ref[k] = src[...].astype(BF16)

    x = x_ref[0].astype(BF16)
    t = x.shape[0]
    zq = _dot(x, w_ref[0])
    zf = _dot(x, w_ref[1])
    zi = _dot(x, w_ref[2])
    zg = _dot(x, w_ref[3])

    qf = _silu(zq)
    e = jnp.exp(-jnp.abs(zf))
    inv1pe = 1.0 / (1.0 + e)
    log_sig = jnp.minimum(zf, 0.0) - jnp.log(1.0 + e)
    sig_neg = jnp.where(zf >= 0.0, e * inv1pe, inv1pe)
    if layer == 0:
        lf = log_sig
        kf = sig_neg
    else:
        p = lbp_ref[...]
        rows = [p[i:i + 1, :] for i in range(p.shape[0])]
        m = functools.reduce(jnp.maximum, rows)
        ex = [jnp.exp(r - m) for r in rows]
        lb = functools.reduce(jnp.add, ex[1:layer + 1]) / functools.reduce(jnp.add, ex)
        la = jnp.log(lb)
        lbv = jnp.log(1.0 - lb) + log_sig
        lf = jnp.maximum(la, lbv) + jnp.log(1.0 + jnp.exp(-jnp.abs(la - lbv)))
        kf = (1.0 - lb) * sig_neg

    row = lax.broadcasted_iota(jnp.int32, (t, 1), 0)
    pos = row & (HG_CHUNK - 1)
    b = lf
    sh = 1
    while sh < HG_CHUNK:
        b = b + jnp.where(pos >= sh, pltpu.roll(b, sh, axis=0), 0.0)
        sh *= 2

    n_chunks = t // HG_CHUNK
    n_heads = HG_PAIR // HG_DIM
    ti = lax.broadcasted_iota(jnp.int32, (HG_CHUNK, HG_CHUNK), 0)
    si = lax.broadcasted_iota(jnp.int32, (HG_CHUNK, HG_CHUNK), 1)

    scores = [[jnp.zeros((HG_CHUNK, HG_CHUNK), F32) for _ in range(n_heads)] for _ in range(n_chunks)]
    for h in HG_LEVELS:
        ref = _block_reference(b, h, row)
        w = jnp.exp(-jnp.abs(b - ref))
        mt = (jnp.where((row & h) != 0, qf, kf) * w).astype(BF16)
        owned = ((ti // (2 * h)) == (si // (2 * h))) & ((ti & h) != 0) & ((si & h) == 0)
        for c in range(n_chunks):
            r0 = c * HG_CHUNK
            for hh in range(n_heads):
                c0 = hh * HG_DIM
                m = mt[r0:r0 + HG_CHUNK, c0:c0 + HG_DIM]
                scores[c][hh] = jnp.where(owned, _dot_nt(m, m), scores[c][hh])

    vb = zi.astype(BF16)
    qk = qf * kf
    qd = (qf * jnp.exp(b)).astype(BF16)
    ng = ng_ref[...]
    gate = _silu(zg)
    for hh in range(n_heads):
        c0 = hh * HG_DIM
        st = state_ref[hh]
        for c in range(n_chunks):
            r0 = c * HG_CHUNK
            rs = slice(r0, r0 + HG_CHUNK)
            cs = slice(c0, c0 + HG_DIM)
            b_c = b[rs, cs]
            b_last = b_c[HG_CHUNK - 1:HG_CHUNK, :]
            v_c = vb[rs, cs]
            o = _dot(scores[c][hh].astype(BF16), v_c)
            o = o + jnp.sum(qk[rs, cs], axis=-1, keepdims=True) * zi[rs, cs]
            o = o + _dot_nt(qd[rs, cs], st.astype(BF16))
            kd = (kf[rs, cs] * jnp.exp(b_last - b_c)).astype(BF16)
            st = jnp.exp(b_last) * st + _dot_tn(v_c, kd)
            o = o * lax.rsqrt(jnp.mean(o * o, axis=-1, keepdims=True) + RMS_EPS)
            o = o * ng[:, cs] * gate[rs, cs]
            o_ref[0, rs, cs] = o.astype(o_ref.dtype)
        state_ref[hh] = st


def _hgrn(layer, xb, w_in, lbp, ng, w_up, w_down):
    bsz, seq, d = xb.shape
    width = HG_HEADS * HG_DIM
    n_pairs = width // HG_PAIR
    ts = min(TOKEN_TILE, seq)
    n_s = seq // ts
    n_steps = bsz * n_pairs * n_s

    def slab_rows(w):
        rows = w.shape[1]
        r = next(r for r in range(BF16_ROWS, rows + 1, BF16_ROWS) if rows % r == 0 and rows // r <= n_steps)
        return r, rows // r

    step = lambda b, p, s: (b * n_pairs + p) * n_s + s

    def slab_specs(w):
        r, n_slabs = slab_rows(w)
        idx = lambda b, p, s: jnp.minimum(step(b, p, s), n_slabs - 1)
        return (pl.BlockSpec((None, r, w.shape[2]), lambda b, p, s: (layer, idx(b, p, s), 0)),
                pl.BlockSpec((r, w.shape[2]), lambda b, p, s: (idx(b, p, s), 0)))

    (up_in, up_out), (down_in, down_out) = slab_specs(w_up), slab_specs(w_down)
    wspec = lambda k: pl.BlockSpec((None, d, HG_PAIR), lambda b, p, s, k=k: (layer, 0, k * n_pairs + p))
    return pl.pallas_call(
        functools.partial(_hgrn_kernel, layer),
        grid=(bsz, n_pairs, n_s),
        in_specs=[
            pl.BlockSpec((1, ts, d), lambda b, p, s: (b, s, 0)),
            wspec(0), wspec(1), wspec(2), wspec(3),
            pl.BlockSpec((lbp.shape[0], HG_PAIR), lambda b, p, s: (0, p)),
            pl.BlockSpec((1, HG_PAIR), lambda b, p, s: (0, p)),
            up_in, down_in,
        ],
        out_specs=[pl.BlockSpec((1, ts, HG_PAIR), lambda b, p, s: (b, s, p)), up_out, down_out],
        out_shape=[jax.ShapeDtypeStruct((bsz, seq, width), BF16),
                   jax.ShapeDtypeStruct(w_up.shape[1:], BF16), jax.ShapeDtypeStruct(w_down.shape[1:], BF16)],
        scratch_shapes=[pltpu.VMEM((4, d, HG_PAIR), BF16), pltpu.VMEM((HG_PAIR // HG_DIM, HG_DIM, HG_DIM), F32)],
        compiler_params=_params(("arbitrary", "arbitrary", "arbitrary")),
        name="hgrn",
    )(xb, w_in, w_in, w_in, w_in, lbp, ng, w_up, w_down)


def _pool_kernel(x_ref, wp_ref, pw_ref, sc_ref, o_ref, w_ref, halo_ref):
    s = pl.program_id(1)

    @pl.when(jnp.logical_and(pl.program_id(0) == 0, s == 0))
    def _():
        w_ref[...] = wp_ref[...].astype(BF16)

    @pl.when(s == 0)
    def _():
        halo_ref[...] = jnp.zeros_like(halo_ref)

    x = x_ref[0].astype(BF16)
    t = x.shape[0]
    p = _dot(x, w_ref[...])
    ext = jnp.concatenate([halo_ref[...], p], axis=0)
    halo_ref[...] = p[t - POOL_HALO:, :]
    avail = s * t + lax.broadcasted_iota(jnp.int32, (t, 1), 0) + 1
    gdim = pw_ref.shape[1]
    for g, w in enumerate(POOL_WINDOWS):
        cs = slice(g * gdim, (g + 1) * gdim)
        acc = ext[:, cs]
        sh = 1
        while sh < w:
            acc = acc + pltpu.roll(acc, sh, axis=0)
            sh *= 2
        cnt = jnp.minimum(avail, w).astype(F32)
        pooled = acc[POOL_HALO:, :] * (1.0 / cnt) - p[:, cs]
        y = _dot(pooled.astype(BF16), pw_ref[g].astype(BF16)) * sc_ref[:, cs]
        o_ref[0, :, cs] = y.astype(o_ref.dtype)


def _pool(layer, xb, w_in, col_block, pw, sc):
    bsz, seq, d = xb.shape
    _, g, gdim, _ = pw.shape
    width = g * gdim
    ts = min(TOKEN_TILE, seq)
    return pl.pallas_call(
        _pool_kernel,
        grid=(bsz, seq // ts),
        in_specs=[
            pl.BlockSpec((1, ts, d), lambda b, s: (b, s, 0)),
            pl.BlockSpec((None, d, width), lambda b, s: (layer, 0, col_block), pipeline_mode=pl.Buffered(1)),
            pl.BlockSpec((None, g, gdim, gdim), lambda b, s: (layer, 0, 0, 0)),
            pl.BlockSpec((1, width), lambda b, s: (0, 0)),
        ],
        out_specs=pl.BlockSpec((1, ts, width), lambda b, s: (b, s, 0)),
        out_shape=jax.ShapeDtypeStruct((bsz, seq, width), BF16),
        scratch_shapes=[pltpu.VMEM((d, width), BF16), pltpu.VMEM((POOL_HALO, width), F32)],
        compiler_params=_params(("arbitrary", "arbitrary")),
        name="pool",
    )(xb, w_in, pw, sc)


def _sgu_kernel(x_ref, wu_ref, wv_ref, g_ref, b_ref, ws_ref, bias_ref, o_ref, w_ref):
    @pl.when(pl.program_id(0) == 0)
    def _():
        w_ref[0] = wu_ref[...].astype(BF16)
        w_ref[1] = wv_ref[...].astype(BF16)

    x = x_ref[...].astype(BF16)
    t = x.shape[0]
    zv = _dot(x, w_ref[1])
    zu = _dot(x, w_ref[0])
    vn = _layer_norm(_gelu_tanh(zv), g_ref[...], b_ref[...]).astype(BF16)
    n_chunks = t // SG_CHUNK
    gdim = vn.shape[1] // SG_GROUPS
    ri = lax.broadcasted_iota(jnp.int32, (SG_CHUNK, SG_CHUNK), 0)
    ci = lax.broadcasted_iota(jnp.int32, (SG_CHUNK, SG_CHUNK), 1)
    for g in range(SG_GROUPS):
        cs = slice(g * gdim, (g + 1) * gdim)
        w = jnp.where(ri >= ci, ws_ref[g], 0.0).astype(BF16)
        rhs = jnp.concatenate([vn[c * SG_CHUNK:(c + 1) * SG_CHUNK, cs] for c in range(n_chunks)], axis=1)
        mixed = _dot(w, rhs)
        for c in range(n_chunks):
            rs = slice(c * SG_CHUNK, (c + 1) * SG_CHUNK)
            y = _gelu_tanh(zu[rs, cs]) * (mixed[:, c * gdim:(c + 1) * gdim] + bias_ref[:, cs])
            o_ref[rs, cs] = y.astype(o_ref.dtype)


def _sgu(layer, xb2, w_in, u_block, v_block, g, b, ws, bias):
    n, d = xb2.shape
    width = g.shape[1]
    ts = min(TOKEN_TILE, n)
    const2 = lambda i: (0, 0)
    return pl.pallas_call(
        _sgu_kernel,
        grid=(n // ts,),
        in_specs=[
            pl.BlockSpec((ts, d), lambda i: (i, 0)),
            pl.BlockSpec((None, d, width), lambda i: (layer, 0, u_block), pipeline_mode=pl.Buffered(1)),
            pl.BlockSpec((None, d, width), lambda i: (layer, 0, v_block), pipeline_mode=pl.Buffered(1)),
            pl.BlockSpec((1, width), const2),
            pl.BlockSpec((1, width), const2),
            pl.BlockSpec((None,) + ws.shape[1:], lambda i: (layer, 0, 0, 0)),
            pl.BlockSpec(bias.shape, const2),
        ],
        out_specs=pl.BlockSpec((ts, width), lambda i: (i, 0)),
        out_shape=jax.ShapeDtypeStruct((n, width), BF16),
        scratch_shapes=[pltpu.VMEM((2, d, width), BF16)],
        compiler_params=_params(("arbitrary",)),
        name="sgu",
    )(xb2, w_in, w_in, g, b, ws, bias)


def _merge_kernel(x_ref, yh_ref, yp_ref, ys_ref, g0_ref, g1_ref, g2_ref, p0_ref, p1_ref, p2_ref, o_ref,
                  gw_ref, pw_ref):
    @pl.when(pl.program_id(1) == 0)
    def _():
        for k, (g_ref, p_ref) in enumerate(((g0_ref, p0_ref), (g1_ref, p1_ref), (g2_ref, p2_ref))):
            gw_ref[k] = g_ref[...].astype(BF16)
            pw_ref[k] = p_ref[...].astype(BF16)

    x = x_ref[...].astype(BF16)
    acc = None
    for k, y_ref in enumerate((yh_ref, yp_ref, ys_ref)):
        term = _sigmoid(_dot(x, gw_ref[k])) * _dot(y_ref[...], pw_ref[k])
        acc = term if acc is None else acc + term
    o_ref[...] = acc.astype(o_ref.dtype)


def _merge(layer, xb2, yh, yp, ys, w_in, gate_col0, ph, pp, ps):
    n, d = xb2.shape
    tm = min(TOKEN_TILE, n)
    tn = MERGE_TN
    gblk0 = gate_col0 // tn
    per_branch = d // tn
    width = ph.shape[1]
    assert pp.shape[1] == width and ps.shape[1] == width
    yspec = lambda y: pl.BlockSpec((tm, y.shape[1]), lambda j, i: (i, 0))
    gspec = lambda k: pl.BlockSpec((None, d, tn), lambda j, i, k=k: (layer, 0, gblk0 + k * per_branch + j))
    pspec = lambda: pl.BlockSpec((None, width, tn), lambda j, i: (layer, 0, j), pipeline_mode=pl.Buffered(1))
    return pl.pallas_call(
        _merge_kernel,
        grid=(d // tn, n // tm),
        in_specs=[pl.BlockSpec((tm, d), lambda j, i: (i, 0)), yspec(yh), yspec(yp), yspec(ys),
                  gspec(0), gspec(1), gspec(2), pspec(), pspec(), pspec()],
        out_specs=pl.BlockSpec((tm, tn), lambda j, i: (i, j)),
        out_shape=jax.ShapeDtypeStruct((n, d), BF16),
        scratch_shapes=[pltpu.VMEM((N_BRANCH, d, tn), BF16), pltpu.VMEM((N_BRANCH, width, tn), BF16)],
        compiler_params=_params(("arbitrary", "arbitrary")),
        name="merge",
    )(xb2, yh, yp, ys, w_in, w_in, w_in, ph, pp, ps)


def _outln_kernel(alpha, m_ref, w_ref, x_ref, g_ref, b_ref, o_ref, ob_ref, wb_ref, mix_ref):
    @pl.when(pl.program_id(0) == 0)
    def _():
        wb_ref[...] = w_ref[...].astype(BF16)

    mix_ref[...] = _dot(m_ref[...], wb_ref[...])
    _residual_layer_norm(alpha, x_ref, mix_ref, g_ref[...], b_ref[...], o_ref, ob_ref)


def _outln(alpha, layer, merged, w_out, x2, g, b):
    n, d = x2.shape
    tm = min(TOKEN_TILE, n)
    const2 = lambda i: (0, 0)
    row = pl.BlockSpec((tm, d), lambda i: (i, 0))
    return pl.pallas_call(
        functools.partial(_outln_kernel, alpha),
        grid=(n // tm,),
        in_specs=[row, pl.BlockSpec((None, d, d), lambda i: (layer, 0, 0), pipeline_mode=pl.Buffered(1)), row,
                  pl.BlockSpec((1, d), const2), pl.BlockSpec((1, d), const2)],
        out_specs=[row, row],
        out_shape=[jax.ShapeDtypeStruct((n, d), F32), jax.ShapeDtypeStruct((n, d), BF16)],
        scratch_shapes=[pltpu.VMEM((d, d), BF16), pltpu.VMEM((tm, d), F32)],
        compiler_params=_params(("arbitrary",)),
        name="outln",
    )(merged, w_out, x2, g, b)


def _causal_conv(h, halo, cw, cb, row):
    n = halo.shape[0]
    h1 = jnp.where(row == 0, halo[n - 1:n, :], pltpu.roll(h, 1, axis=0))
    h2 = jnp.where(row == 0, halo[n - 2:n - 1, :], jnp.where(row == 1, halo[n - 1:n, :], pltpu.roll(h, 2, axis=0)))
    return cb + cw[0:1, :] * h2 + cw[1:2, :] * h1 + cw[2:3, :] * h


def _ffn_kernel(alpha, tiles_per_seq, xb_ref, xh_ref, wa_ref, wb_ref, cwa_ref, cwb_ref, cba_ref, cbb_ref,
                wd_ref, x_ref, g_ref, b_ref, o_ref, ob_ref, acc_ref):
    i = pl.program_id(0)
    j = pl.program_id(1)

    @pl.when(j == 0)
    def _():
        acc_ref[...] = jnp.zeros_like(acc_ref)

    xb = xb_ref[...]
    xh = jnp.where(i % tiles_per_seq == 0, jnp.zeros_like(xh_ref), xh_ref[...])
    row = lax.broadcasted_iota(jnp.int32, (xb.shape[0], 1), 0)
    gates = []
    for c0 in range(0, wa_ref.shape[1], FFN_COLS):
        cs = pl.ds(c0, FFN_COLS)
        wa = wa_ref[:, cs]
        wb = wb_ref[:, cs]
        a = _causal_conv(_dot(xb, wa), _dot(xh, wa), cwa_ref[:, cs], cba_ref[:, cs], row)
        b = _causal_conv(_dot(xb, wb), _dot(xh, wb), cwb_ref[:, cs], cbb_ref[:, cs], row)
        gates.append((_silu(a) * b).astype(BF16))
    acc_ref[...] += _dot(jnp.concatenate(gates, axis=1), wd_ref[...])

    @pl.when(j == pl.num_programs(1) - 1)
    def _():
        _residual_layer_norm(alpha, x_ref, acc_ref, g_ref[...], b_ref[...], o_ref, ob_ref)


def _ffn(alpha, seq, xb2, x2, w_up, conv_w, conv_b, w_down, g, b):
    n, d = x2.shape
    f = w_down.shape[0]
    tm = min(FFN_TM, seq)
    tf = FFN_TF
    nf = f // tf
    halo_blocks = tm // BF16_ROWS
    row = pl.BlockSpec((tm, d), lambda i, j: (i, 0))
    const2 = lambda i, j: (0, 0)
    return pl.pallas_call(
        functools.partial(_ffn_kernel, alpha, seq // tm),
        grid=(n // tm, nf),
        in_specs=[
            row,
            pl.BlockSpec((BF16_ROWS, d), lambda i, j: (jnp.maximum(i * halo_blocks - 1, 0), 0)),
            pl.BlockSpec((d, tf), lambda i, j: (0, j)),
            pl.BlockSpec((d, tf), lambda i, j: (0, nf + j)),
            pl.BlockSpec((CONV_W, tf), lambda i, j: (0, j)),
            pl.BlockSpec((CONV_W, tf), lambda i, j: (0, nf + j)),
            pl.BlockSpec((1, tf), lambda i, j: (0, j)),
            pl.BlockSpec((1, tf), lambda i, j: (0, nf + j)),
            pl.BlockSpec((tf, d), lambda i, j: (j, 0)),
            row,
            pl.BlockSpec((1, d), const2),
            pl.BlockSpec((1, d), const2),
        ],
        out_specs=[row, row],
        out_shape=[jax.ShapeDtypeStruct((n, d), F32), jax.ShapeDtypeStruct((n, d), BF16)],
        scratch_shapes=[pltpu.VMEM((tm, d), F32)],
        compiler_params=_params(("parallel", "arbitrary")),
        name="ffn",
    )(xb2, xb2, w_up, w_up, conv_w, conv_w, conv_b, conv_b, w_down, x2, g, b)


def kernel(x, w_in, hg_lower_bounds, hg_norm_g, pool_w, pool_scale, sg_ln_g, sg_ln_b, sg_w, sg_b, w_hg_proj, w_pool_proj, w_sg_proj, w_out, ln1_g, ln1_b, w_up, conv_w, conv_b, w_down, ln2_g, ln2_b):
    bsz, seq, d = x.shape
    depth = w_in.shape[0]
    n = bsz * seq
    alpha = (2 * depth) ** 0.25
    hg_width = HG_HEADS * HG_DIM
    pool_width = pool_w.shape[1] * pool_w.shape[2]
    sg_width = sg_ln_g.shape[1]
    pool_col = 4 * hg_width
    u_col = pool_col + pool_width
    v_col = u_col + sg_width
    gate_col = v_col + sg_width
    assert pool_col % pool_width == 0 and u_col % sg_width == 0 and gate_col % MERGE_TN == 0

    x2 = x.reshape(n, d)
    xb2 = x2
    lbp = hg_lower_bounds.astype(F32)
    for l in range(depth):
        xb3 = xb2.reshape(bsz, seq, d)
        y_hg, w_up_b, w_down_b = _hgrn(l, xb3, w_in, lbp, hg_norm_g[l].reshape(1, hg_width), w_up, w_down)
        y_hg = y_hg.reshape(n, hg_width)
        y_pool = _pool(l, xb3, w_in, pool_col // pool_width, pool_w,
                       pool_scale[l].reshape(1, pool_width)).reshape(n, pool_width)
        gdim = sg_width // SG_GROUPS
        bias = jnp.repeat(sg_b[l].T, gdim, axis=1)
        y_sg = _sgu(l, xb2, w_in, u_col // sg_width, v_col // sg_width, sg_ln_g[l].reshape(1, sg_width),
                    sg_ln_b[l].reshape(1, sg_width), sg_w, bias)
        merged = _merge(l, xb2, y_hg, y_pool, y_sg, w_in, gate_col, w_hg_proj, w_pool_proj, w_sg_proj)
        x2, xb2 = _outln(alpha, l, merged, w_out, x2, ln1_g[l].reshape(1, d), ln1_b[l].reshape(1, d))
        x2, xb2 = _ffn(alpha, seq, xb2, x2, w_up_b, conv_w[l], conv_b[l].reshape(1, -1), w_down_b,
                       ln2_g[l].reshape(1, d), ln2_b[l].reshape(1, d))
    return x2.reshape(bsz, seq, d)
```

```python
import functools
import math

import jax
import jax.numpy as jnp
from jax import lax
from jax.experimental import pallas as pl
from jax.experimental.pallas import tpu as pltpu

F32 = jnp.float32
BF16 = jnp.bfloat16

HG_HEADS = 8
HG_DIM = 128
POOL_WINDOWS = (2, 4, 8, 16)
SG_GROUPS = 8
SG_CHUNK = 128
N_BRANCH = 3
CONV_W = 3
LN_EPS = 1e-5
RMS_EPS = 1e-6

LANES = 128
SUBLANES = 8
BF16_ROWS = 16
VMEM_LIMIT_BYTES = 60 * 1024 * 1024

HG_PAIR = 2 * HG_DIM
HG_CHUNK = 128
HG_LEVELS = tuple(HG_CHUNK >> (i + 1) for i in range(int(math.log2(HG_CHUNK))))
POOL_HALO = 16
TOKEN_TILE = 512
HG_SUB = 256
MERGE_TN = 512
FFN_TM = 512
FFN_TF = 512
FFN_COLS = 256
LN_ROWS = 128


def _params(semantics):
    return pltpu.CompilerParams(dimension_semantics=semantics, vmem_limit_bytes=VMEM_LIMIT_BYTES)


def _dot(a, b):
    return jnp.dot(a, b, preferred_element_type=F32)


def _dot_nt(a, b):
    return lax.dot_general(a, b, (((1,), (1,)), ((), ())), preferred_element_type=F32)


def _dot_tn(a, b):
    return lax.dot_general(a, b, (((0,), (0,)), ((), ())), preferred_element_type=F32)


def _sigmoid(x):
    return 1.0 / (1.0 + jnp.exp(-x))


def _silu(x):
    return x * _sigmoid(x)


def _gelu_tanh(x):
    c = math.sqrt(2.0 / math.pi)
    return 0.5 * x * (1.0 + jnp.tanh(c * (x + 0.044715 * (x * x * x))))


def _layer_norm(y, g, b):
    mu = jnp.mean(y, axis=-1, keepdims=True)
    d = y - mu
    var = jnp.mean(d * d, axis=-1, keepdims=True)
    return d * lax.rsqrt(var + LN_EPS) * g + b


def _residual_layer_norm(alpha, x_ref, y_ref, g, b, o_ref, ob_ref):
    def body(i, carry):
        rs = pl.ds(pl.multiple_of(i * LN_ROWS, LN_ROWS), LN_ROWS)
        out = _layer_norm(alpha * x_ref[rs, :] + y_ref[rs, :], g, b)
        o_ref[rs, :] = out
        ob_ref[rs, :] = out.astype(ob_ref.dtype)
        return carry
    lax.fori_loop(0, x_ref.shape[0] // LN_ROWS, body, 0)


def _block_reference(b, h, row):
    t, c = b.shape
    if 2 * h >= SUBLANES:
        blk = b.reshape(t // (2 * h), 2 * h, c)
        ref = jnp.broadcast_to(blk[:, h - 1:h, :], blk.shape)
        return ref.reshape(t, c)
    pos = row & (2 * h - 1)
    out = b
    for p in range(2 * h):
        off = p - (h - 1)
        if off == 0:
            continue
        shifted = pltpu.roll(b, off % t, axis=0)
        out = jnp.where(pos == p, shifted, out)
    return out


def _hgrn_kernel(layer, x_ref, wq_ref, wf_ref, wi_ref, wg_ref, lbp_ref, ng_ref, up_ref, down_ref,
                 o_ref, upb_ref, downb_ref, w_ref, state_ref):
    upb_ref[...] = up_ref[...].astype(BF16)
    downb_ref[...] = down_ref[...].astype(BF16)

    @pl.when(pl.program_id(2) == 0)
    def _():
        state_ref[...] = jnp.zeros_like(state_ref)
        for k, src in enumerate((wq_ref, wf_ref, wi_ref, wg_ref)):
            w_ref[k] = src[...].astype(BF16)

    bases = range(0, x_ref.shape[1], HG_SUB)
    zs = []
    for base in bases:
        x = x_ref[0, pl.ds(base, HG_SUB), :].astype(BF16)
        zs.append([_dot(x, w_ref[k]) for k in range(4)])
    states = [state_ref[hh] for hh in range(HG_PAIR // HG_DIM)]
    for base, (zq, zf, zi, zg) in zip(bases, zs):
        t = HG_SUB

        qf = _silu(zq)
        e = jnp.exp(-jnp.abs(zf))
        inv1pe = 1.0 / (1.0 + e)
        log_sig = jnp.minimum(zf, 0.0) - jnp.log(1.0 + e)
        sig_neg = jnp.where(zf >= 0.0, e * inv1pe, inv1pe)
        if layer == 0:
            lf = log_sig
            kf = sig_neg
        else:
            p = lbp_ref[...]
            rows = [p[i:i + 1, :] for i in range(p.shape[0])]
            m = functools.reduce(jnp.maximum, rows)
            ex = [jnp.exp(r - m) for r in rows]
            lb = functools.reduce(jnp.add, ex[1:layer + 1]) / functools.reduce(jnp.add, ex)
            la = jnp.log(lb)
            lbv = jnp.log(1.0 - lb) + log_sig
            lf = jnp.maximum(la, lbv) + jnp.log(1.0 + jnp.exp(-jnp.abs(la - lbv)))
            kf = (1.0 - lb) * sig_neg

        row = lax.broadcasted_iota(jnp.int32, (t, 1), 0)
        pos = row & (HG_CHUNK - 1)
        b = lf
        sh = 1
        while sh < HG_CHUNK:
            b = b + jnp.where(pos >= sh, pltpu.roll(b, sh, axis=0), 0.0)
            sh *= 2

        n_chunks = t // HG_CHUNK
        n_heads = HG_PAIR // HG_DIM
        ti = lax.broadcasted_iota(jnp.int32, (HG_CHUNK, HG_CHUNK), 0)
        si = lax.broadcasted_iota(jnp.int32, (HG_CHUNK, HG_CHUNK), 1)

        scores = [[jnp.zeros((HG_CHUNK, HG_CHUNK), F32) for _ in range(n_heads)] for _ in range(n_chunks)]
        for h in HG_LEVELS:
            ref = _block_reference(b, h, row)
            w = jnp.exp(-jnp.abs(b - ref))
            mt = (jnp.where((row & h) != 0, qf, kf) * w).astype(BF16)
            owned = ((ti // (2 * h)) == (si // (2 * h))) & ((ti & h) != 0) & ((si & h) == 0)
            for c in range(n_chunks):
                r0 = c * HG_CHUNK
                for hh in range(n_heads):
                    c0 = hh * HG_DIM
                    m = mt[r0:r0 + HG_CHUNK, c0:c0 + HG_DIM]
                    scores[c][hh] = jnp.where(owned, _dot_nt(m, m), scores[c][hh])

        vb = zi.astype(BF16)
        qk = qf * kf
        qd = (qf * jnp.exp(b)).astype(BF16)
        ng = ng_ref[...]
        gate = _silu(zg)
        for hh in range(n_heads):
            c0 = hh * HG_DIM
            st = states[hh]
            for c in range(n_chunks):
                r0 = c * HG_CHUNK
                rs = slice(r0, r0 + HG_CHUNK)
                cs = slice(c0, c0 + HG_DIM)
                b_c = b[rs, cs]
                b_last = b_c[HG_CHUNK - 1:HG_CHUNK, :]
                v_c = vb[rs, cs]
                o = _dot(scores[c][hh].astype(BF16), v_c)
                o = o + jnp.sum(qk[rs, cs], axis=-1, keepdims=True) * zi[rs, cs]
                o = o + _dot_nt(qd[rs, cs], st.astype(BF16))
                kd = (kf[rs, cs] * jnp.exp(b_last - b_c)).astype(BF16)
                st = jnp.exp(b_last) * st + _dot_tn(v_c, kd)
                o = o * lax.rsqrt(jnp.mean(o * o, axis=-1, keepdims=True) + RMS_EPS)
                o = o * ng[:, cs] * gate[rs, cs]
                o_ref[0, pl.ds(base + r0, HG_CHUNK), cs] = o.astype(o_ref.dtype)
            states[hh] = st

    for hh, st in enumerate(states):
        state_ref[hh] = st


def _hgrn(layer, xb, w_in, lbp, ng, w_up, w_down):
    bsz, seq, d = xb.shape
    width = HG_HEADS * HG_DIM
    n_pairs = width // HG_PAIR
    ts = min(TOKEN_TILE, seq)
    n_s = seq // ts
    n_steps = bsz * n_pairs * n_s

    def slab_rows(w):
        rows = w.shape[1]
        r = next(r for r in range(BF16_ROWS, rows + 1, BF16_ROWS) if rows % r == 0 and rows // r <= n_steps)
        return r, rows // r

    step = lambda b, p, s: (b * n_pairs + p) * n_s + s

    def slab_specs(w):
        r, n_slabs = slab_rows(w)
        idx = lambda b, p, s: jnp.minimum(step(b, p, s), n_slabs - 1)
        return (pl.BlockSpec((None, r, w.shape[2]), lambda b, p, s: (layer, idx(b, p, s), 0)),
                pl.BlockSpec((r, w.shape[2]), lambda b, p, s: (idx(b, p, s), 0)))

    (up_in, up_out), (down_in, down_out) = slab_specs(w_up), slab_specs(w_down)
    wspec = lambda k: pl.BlockSpec((None, d, HG_PAIR), lambda b, p, s, k=k: (layer, 0, k * n_pairs + p))
    return pl.pallas_call(
        functools.partial(_hgrn_kernel, layer),
        grid=(bsz, n_pairs, n_s),
        in_specs=[
            pl.BlockSpec((1, ts, d), lambda b, p, s: (b, s, 0)),
            wspec(0), wspec(1), wspec(2), wspec(3),
            pl.BlockSpec((lbp.shape[0], HG_PAIR), lambda b, p, s: (0, p)),
            pl.BlockSpec((1, HG_PAIR), lambda b, p, s: (0, p)),
            up_in, down_in,
        ],
        out_specs=[pl.BlockSpec((1, ts, HG_PAIR), lambda b, p, s: (b, s, p)), up_out, down_out],
        out_shape=[jax.ShapeDtypeStruct((bsz, seq, width), BF16),
                   jax.ShapeDtypeStruct(w_up.shape[1:], BF16), jax.ShapeDtypeStruct(w_down.shape[1:], BF16)],
        scratch_shapes=[pltpu.VMEM((4, d, HG_PAIR), BF16), pltpu.VMEM((HG_PAIR // HG_DIM, HG_DIM, HG_DIM), F32)],
        compiler_params=_params(("arbitrary", "arbitrary", "arbitrary")),
        name="hgrn",
    )(xb, w_in, w_in, w_in, w_in, lbp, ng, w_up, w_down)


def _pool_kernel(x_ref, wp_ref, pw_ref, sc_ref, o_ref, w_ref, halo_ref):
    s = pl.program_id(1)

    @pl.when(jnp.logical_and(pl.program_id(0) == 0, s == 0))
    def _():
        w_ref[...] = wp_ref[...].astype(BF16)

    @pl.when(s == 0)
    def _():
        halo_ref[...] = jnp.zeros_like(halo_ref)

    x = x_ref[0].astype(BF16)
    t = x.shape[0]
    p = _dot(x, w_ref[...])
    ext = jnp.concatenate([halo_ref[...], p], axis=0)
    halo_ref[...] = p[t - POOL_HALO:, :]
    avail = s * t + lax.broadcasted_iota(jnp.int32, (t, 1), 0) + 1
    gdim = pw_ref.shape[1]
    for g, w in enumerate(POOL_WINDOWS):
        cs = slice(g * gdim, (g + 1) * gdim)
        acc = ext[:, cs]
        sh = 1
        while sh < w:
            acc = acc + pltpu.roll(acc, sh, axis=0)
            sh *= 2
        cnt = jnp.minimum(avail, w).astype(F32)
        pooled = acc[POOL_HALO:, :] * (1.0 / cnt) - p[:, cs]
        y = _dot(pooled.astype(BF16), pw_ref[g].astype(BF16)) * sc_ref[:, cs]
        o_ref[0, :, cs] = y.astype(o_ref.dtype)


def _pool(layer, xb, w_in, col_block, pw, sc):
    bsz, seq, d = xb.shape
    _, g, gdim, _ = pw.shape
    width = g * gdim
    ts = min(TOKEN_TILE, seq)
    return pl.pallas_call(
        _pool_kernel,
        grid=(bsz, seq // ts),
        in_specs=[
            pl.BlockSpec((1, ts, d), lambda b, s: (b, s, 0)),
            pl.BlockSpec((None, d, width), lambda b, s: (layer, 0, col_block), pipeline_mode=pl.Buffered(1)),
            pl.BlockSpec((None, g, gdim, gdim), lambda b, s: (layer, 0, 0, 0)),
            pl.BlockSpec((1, width), lambda b, s: (0, 0)),
        ],
        out_specs=pl.BlockSpec((1, ts, width), lambda b, s: (b, s, 0)),
        out_shape=jax.ShapeDtypeStruct((bsz, seq, width), BF16),
        scratch_shapes=[pltpu.VMEM((d, width), BF16), pltpu.VMEM((POOL_HALO, width), F32)],
        compiler_params=_params(("arbitrary", "arbitrary")),
        name="pool",
    )(xb, w_in, pw, sc)


def _sgu_kernel(x_ref, wu_ref, wv_ref, g_ref, b_ref, ws_ref, bias_ref, o_ref, w_ref):
    @pl.when(pl.program_id(0) == 0)
    def _():
        w_ref[0] = wu_ref[...].astype(BF16)
        w_ref[1] = wv_ref[...].astype(BF16)

    x = x_ref[...].astype(BF16)
    t = x.shape[0]
    zv = _dot(x, w_ref[1])
    zu = _dot(x, w_ref[0])
    vn = _layer_norm(_gelu_tanh(zv), g_ref[...], b_ref[...]).astype(BF16)
    n_chunks = t // SG_CHUNK
    gdim = vn.shape[1] // SG_GROUPS
    ri = lax.broadcasted_iota(jnp.int32, (SG_CHUNK, SG_CHUNK), 0)
    ci = lax.broadcasted_iota(jnp.int32, (SG_CHUNK, SG_CHUNK), 1)
    for g in range(SG_GROUPS):
        cs = slice(g * gdim, (g + 1) * gdim)
        w = jnp.where(ri >= ci, ws_ref[g], 0.0).astype(BF16)
        rhs = jnp.concatenate([vn[c * SG_CHUNK:(c + 1) * SG_CHUNK, cs] for c in range(n_chunks)], axis=1)
        mixed = _dot(w, rhs)
        for c in range(n_chunks):
            rs = slice(c * SG_CHUNK, (c + 1) * SG_CHUNK)
            y = _gelu_tanh(zu[rs, cs]) * (mixed[:, c * gdim:(c + 1) * gdim] + bias_ref[:, cs])
            o_ref[rs, cs] = y.astype(o_ref.dtype)


def _sgu(layer, xb2, w_in, u_block, v_block, g, b, ws, bias):
    n, d = xb2.shape
    width = g.shape[1]
    ts = min(TOKEN_TILE, n)
    const2 = lambda i: (0, 0)
    return pl.pallas_call(
        _sgu_kernel,
        grid=(n // ts,),
        in_specs=[
            pl.BlockSpec((ts, d), lambda i: (i, 0)),
            pl.BlockSpec((None, d, width), lambda i: (layer, 0, u_block), pipeline_mode=pl.Buffered(1)),
            pl.BlockSpec((None, d, width), lambda i: (layer, 0, v_block), pipeline_mode=pl.Buffered(1)),
            pl.BlockSpec((1, width), const2),
            pl.BlockSpec((1, width), const2),
            pl.BlockSpec((None,) + ws.shape[1:], lambda i: (layer, 0, 0, 0)),
            pl.BlockSpec(bias.shape, const2),
        ],
        out_specs=pl.BlockSpec((ts, width), lambda i: (i, 0)),
        out_shape=jax.ShapeDtypeStruct((n, width), BF16),
        scratch_shapes=[pltpu.VMEM((2, d, width), BF16)],
        compiler_params=_params(("arbitrary",)),
        name="sgu",
    )(xb2, w_in, w_in, g, b, ws, bias)


def _merge_kernel(x_ref, yh_ref, yp_ref, ys_ref, g0_ref, g1_ref, g2_ref, p0_ref, p1_ref, p2_ref, o_ref,
                  gw_ref, pw_ref):
    @pl.when(pl.program_id(1) == 0)
    def _():
        for k, (g_ref, p_ref) in enumerate(((g0_ref, p0_ref), (g1_ref, p1_ref), (g2_ref, p2_ref))):
            gw_ref[k] = g_ref[...].astype(BF16)
            pw_ref[k] = p_ref[...].astype(BF16)

    x = x_ref[...].astype(BF16)
    acc = None
    for k, y_ref in enumerate((yh_ref, yp_ref, ys_ref)):
        term = _sigmoid(_dot(x, gw_ref[k])) * _dot(y_ref[...], pw_ref[k])
        acc = term if acc is None else acc + term
    o_ref[...] = acc.astype(o_ref.dtype)


def _merge(layer, xb2, yh, yp, ys, w_in, gate_col0, ph, pp, ps):
    n, d = xb2.shape
    tm = min(TOKEN_TILE, n)
    tn = MERGE_TN
    gblk0 = gate_col0 // tn
    per_branch = d // tn
    width = ph.shape[1]
    assert pp.shape[1] == width and ps.shape[1] == width
    yspec = lambda y: pl.BlockSpec((tm, y.shape[1]), lambda j, i: (i, 0))
    gspec = lambda k: pl.BlockSpec((None, d, tn), lambda j, i, k=k: (layer, 0, gblk0 + k * per_branch + j))
    pspec = lambda: pl.BlockSpec((None, width, tn), lambda j, i: (layer, 0, j), pipeline_mode=pl.Buffered(1))
    return pl.pallas_call(
        _merge_kernel,
        grid=(d // tn, n // tm),
        in_specs=[pl.BlockSpec((tm, d), lambda j, i: (i, 0)), yspec(yh), yspec(yp), yspec(ys),
                  gspec(0), gspec(1), gspec(2), pspec(), pspec(), pspec()],
        out_specs=pl.BlockSpec((tm, tn), lambda j, i: (i, j)),
        out_shape=jax.ShapeDtypeStruct((n, d), BF16),
        scratch_shapes=[pltpu.VMEM((N_BRANCH, d, tn), BF16), pltpu.VMEM((N_BRANCH, width, tn), BF16)],
        compiler_params=_params(("arbitrary", "arbitrary")),
        name="merge",
    )(xb2, yh, yp, ys, w_in, w_in, w_in, ph, pp, ps)


def _outln_kernel(alpha, m_ref, w_ref, x_ref, g_ref, b_ref, o_ref, ob_ref, wb_ref, mix_ref):
    @pl.when(pl.program_id(0) == 0)
    def _():
        wb_ref[...] = w_ref[...].astype(BF16)

    mix_ref[...] = _dot(m_ref[...], wb_ref[...])
    _residual_layer_norm(alpha, x_ref, mix_ref, g_ref[...], b_ref[...], o_ref, ob_ref)


def _outln(alpha, layer, merged, w_out, x2, g, b):
    n, d = x2.shape
    tm = min(TOKEN_TILE, n)
    const2 = lambda i: (0, 0)
    row = pl.BlockSpec((tm, d), lambda i: (i, 0))
    return pl.pallas_call(
        functools.partial(_outln_kernel, alpha),
        grid=(n // tm,),
        in_specs=[row, pl.BlockSpec((None, d, d), lambda i: (layer, 0, 0), pipeline_mode=pl.Buffered(1)), row,
                  pl.BlockSpec((1, d), const2), pl.BlockSpec((1, d), const2)],
        out_specs=[row, row],
        out_shape=[jax.ShapeDtypeStruct((n, d), F32), jax.ShapeDtypeStruct((n, d), BF16)],
        scratch_shapes=[pltpu.VMEM((d, d), BF16), pltpu.VMEM((tm, d), F32)],
        compiler_params=_params(("arbitrary",)),
        name="outln",
    )(merged, w_out, x2, g, b)


def _causal_conv(h, halo, cw, cb, row):
    n = halo.shape[0]
    h1 = jnp.where(row == 0, halo[n - 1:n, :], pltpu.roll(h, 1, axis=0))
    h2 = jnp.where(row == 0, halo[n - 2:n - 1, :], jnp.where(row == 1, halo[n - 1:n, :], pltpu.roll(h, 2, axis=0)))
    return cb + cw[0:1, :] * h2 + cw[1:2, :] * h1 + cw[2:3, :] * h


def _ffn_kernel(alpha, tiles_per_seq, xb_ref, xh_ref, wa_ref, wb_ref, cwa_ref, cwb_ref, cba_ref, cbb_ref,
                wd_ref, x_ref, g_ref, b_ref, o_ref, ob_ref, acc_ref):
    i = pl.program_id(0)
    j = pl.program_id(1)

    @pl.when(j == 0)
    def _():
        acc_ref[...] = jnp.zeros_like(acc_ref)

    xb = xb_ref[...]
    xh = jnp.where(i % tiles_per_seq == 0, jnp.zeros_like(xh_ref), xh_ref[...])
    row = lax.broadcasted_iota(jnp.int32, (xb.shape[0], 1), 0)
    gates = []
    for c0 in range(0, wa_ref.shape[1], FFN_COLS):
        cs = pl.ds(c0, FFN_COLS)
        wa = wa_ref[:, cs]
        wb = wb_ref[:, cs]
        a = _causal_conv(_dot(xb, wa), _dot(xh, wa), cwa_ref[:, cs], cba_ref[:, cs], row)
        b = _causal_conv(_dot(xb, wb), _dot(xh, wb), cwb_ref[:, cs], cbb_ref[:, cs], row)
        gates.append((_silu(a) * b).astype(BF16))
    acc_ref[...] += _dot(jnp.concatenate(gates, axis=1), wd_ref[...])

    @pl.when(j == pl.num_programs(1) - 1)
    def _():
        _residual_layer_norm(alpha, x_ref, acc_ref, g_ref[...], b_ref[...], o_ref, ob_ref)


def _ffn(alpha, seq, xb2, x2, w_up, conv_w, conv_b, w_down, g, b):
    n, d = x2.shape
    f = w_down.shape[0]
    tm = min(FFN_TM, seq)
    tf = FFN_TF
    nf = f // tf
    halo_blocks = tm // BF16_ROWS
    row = pl.BlockSpec((tm, d), lambda i, j: (i, 0))
    const2 = lambda i, j: (0, 0)
    return pl.pallas_call(
        functools.partial(_ffn_kernel, alpha, seq // tm),
        grid=(n // tm, nf),
        in_specs=[
            row,
            pl.BlockSpec((BF16_ROWS, d), lambda i, j: (jnp.maximum(i * halo_blocks - 1, 0), 0)),
            pl.BlockSpec((d, tf), lambda i, j: (0, j)),
            pl.BlockSpec((d, tf), lambda i, j: (0, nf + j)),
            pl.BlockSpec((CONV_W, tf), lambda i, j: (0, j)),
            pl.BlockSpec((CONV_W, tf), lambda i, j: (0, nf + j)),
            pl.BlockSpec((1, tf), lambda i, j: (0, j)),
            pl.BlockSpec((1, tf), lambda i, j: (0, nf + j)),
            pl.BlockSpec((tf, d), lambda i, j: (j, 0)),
            row,
            pl.BlockSpec((1, d), const2),
            pl.BlockSpec((1, d), const2),
        ],
        out_specs=[row, row],
        out_shape=[jax.ShapeDtypeStruct((n, d), F32), jax.ShapeDtypeStruct((n, d), BF16)],
        scratch_shapes=[pltpu.VMEM((tm, d), F32)],
        compiler_params=_params(("parallel", "arbitrary")),
        name="ffn",
    )(xb2, xb2, w_up, w_up, conv_w, conv_w, conv_b, conv_b, w_down, x2, g, b)


def kernel(x, w_in, hg_lower_bounds, hg_norm_g, pool_w, pool_scale, sg_ln_g, sg_ln_b, sg_w, sg_b, w_hg_proj, w_pool_proj, w_sg_proj, w_out, ln1_g, ln1_b, w_up, conv_w, conv_b, w_down, ln2_g, ln2_b):
    bsz, seq, d = x.shape
    depth = w_in.shape[0]
    n = bsz * seq
    alpha = (2 * depth) ** 0.25
    hg_width = HG_HEADS * HG_DIM
    pool_width = pool_w.shape[1] * pool_w.shape[2]
    sg_width = sg_ln_g.shape[1]
    pool_col = 4 * hg_width
    u_col = pool_col + pool_width
    v_col = u_col + sg_width
    gate_col = v_col + sg_width
    assert pool_col % pool_width == 0 and u_col % sg_width == 0 and gate_col % MERGE_TN == 0

    x2 = x.reshape(n, d)
    xb2 = x2
    lbp = hg_lower_bounds.astype(F32)
    for l in range(depth):
        xb3 = xb2.reshape(bsz, seq, d)
        y_hg, w_up_b, w_down_b = _hgrn(l, xb3, w_in, lbp, hg_norm_g[l].reshape(1, hg_width), w_up, w_down)
        y_hg = y_hg.reshape(n, hg_width)
        y_pool = _pool(l, xb3, w_in, pool_col // pool_width, pool_w,
                       pool_scale[l].reshape(1, pool_width)).reshape(n, pool_width)
        gdim = sg_width // SG_GROUPS
        bias = jnp.repeat(sg_b[l].T, gdim, axis=1)
        y_sg = _sgu(l, xb2, w_in, u_col // sg_width, v_col // sg_width, sg_ln_g[l].reshape(1, sg_width),
                    sg_ln_b[l].reshape(1, sg_width), sg_w, bias)
        merged = _merge(l, xb2, y_hg, y_pool, y_sg, w_in, gate_col, w_hg_proj, w_pool_proj, w_sg_proj)
        x2, xb2 = _outln(alpha, l, merged, w_out, x2, ln1_g[l].reshape(1, d), ln1_b[l].reshape(1, d))
        x2, xb2 = _ffn(alpha, seq, xb2, x2, w_up_b, conv_w[l], conv_b[l].reshape(1, -1), w_down_b,
                       ln2_g[l].reshape(1, d), ln2_b[l].reshape(1, d))
    return x2.reshape(bsz, seq, d)
```

```python
import functools
import math

import jax
import jax.numpy as jnp
from jax import lax
from jax.experimental import pallas as pl
from jax.experimental.pallas import tpu as pltpu

F32 = jnp.float32
BF16 = jnp.bfloat16

HG_HEADS = 8
HG_DIM = 128
POOL_WINDOWS = (2, 4, 8, 16)
SG_GROUPS = 8
SG_CHUNK = 128
N_BRANCH = 3
CONV_W = 3
LN_EPS = 1e-5
RMS_EPS = 1e-6

LANES = 128
SUBLANES = 8
BF16_ROWS = 16
VMEM_LIMIT_BYTES = 60 * 1024 * 1024

HG_PAIR = 2 * HG_DIM
HG_CHUNK = 128
HG_LEVELS = tuple(HG_CHUNK >> (i + 1) for i in range(int(math.log2(HG_CHUNK))))
POOL_HALO = 16
TOKEN_TILE = 512
HG_SUB = 128
MERGE_TN = 512
FFN_TM = 512
FFN_TF = 512
FFN_COLS = 256
LN_ROWS = 128


def _params(semantics):
    return pltpu.CompilerParams(dimension_semantics=semantics, vmem_limit_bytes=VMEM_LIMIT_BYTES)


def _dot(a, b):
    return jnp.dot(a, b, preferred_element_type=F32)


def _dot_nt(a, b):
    return lax.dot_general(a, b, (((1,), (1,)), ((), ())), preferred_element_type=F32)


def _dot_tn(a, b):
    return lax.dot_general(a, b, (((0,), (0,)), ((), ())), preferred_element_type=F32)


def _sigmoid(x):
    return 1.0 / (1.0 + jnp.exp(-x))


def _silu(x):
    return x * _sigmoid(x)


def _gelu_tanh(x):
    c = math.sqrt(2.0 / math.pi)
    return 0.5 * x * (1.0 + jnp.tanh(c * (x + 0.044715 * (x * x * x))))


def _layer_norm(y, g, b):
    mu = jnp.mean(y, axis=-1, keepdims=True)
    d = y - mu
    var = jnp.mean(d * d, axis=-1, keepdims=True)
    return d * lax.rsqrt(var + LN_EPS) * g + b


def _residual_layer_norm(alpha, x_ref, y_ref, g, b, o_ref, ob_ref):
    def body(i, carry):
        rs = pl.ds(pl.multiple_of(i * LN_ROWS, LN_ROWS), LN_ROWS)
        out = _layer_norm(alpha * x_ref[rs, :] + y_ref[rs, :], g, b)
        o_ref[rs, :] = out
        ob_ref[rs, :] = out.astype(ob_ref.dtype)
        return carry
    lax.fori_loop(0, x_ref.shape[0] // LN_ROWS, body, 0)


def _block_reference(b, h, row):
    t, c = b.shape
    if 2 * h >= SUBLANES:
        blk = b.reshape(t // (2 * h), 2 * h, c)
        ref = jnp.broadcast_to(blk[:, h - 1:h, :], blk.shape)
        return ref.reshape(t, c)
    pos = row & (2 * h - 1)
    out = b
    for p in range(2 * h):
        off = p - (h - 1)
        if off == 0:
            continue
        shifted = pltpu.roll(b, off % t, axis=0)
        out = jnp.where(pos == p, shifted, out)
    return out


def _hgrn_kernel(layer, x_ref, wq_ref, wf_ref, wi_ref, wg_ref, lbp_ref, ng_ref, up_ref, down_ref,
                 o_ref, upb_ref, downb_ref, w_ref, state_ref):
    upb_ref[...] = up_ref[...].astype(BF16)
    downb_ref[...] = down_ref[...].astype(BF16)

    @pl.when(pl.program_id(2) == 0)
    def _():
        state_ref[...] = jnp.zeros_like(state_ref)
        for k, src in enumerate((wq_ref, wf_ref, wi_ref, wg_ref)):
            w_ref[k] = src[...].astype(BF16)

    bases = range(0, x_ref.shape[1], HG_SUB)
    zs = []
    for base in bases:
        x = x_ref[0, pl.ds(base, HG_SUB), :].astype(BF16)
        zs.append([_dot(x, w_ref[k]) for k in range(4)])
    states = [state_ref[hh] for hh in range(HG_PAIR // HG_DIM)]
    for base, (zq, zf, zi, zg) in zip(bases, zs):
        t = HG_SUB

        qf = _silu(zq)
        e = jnp.exp(-jnp.abs(zf))
        inv1pe = 1.0 / (1.0 + e)
        log_sig = jnp.minimum(zf, 0.0) - jnp.log(1.0 + e)
        sig_neg = jnp.where(zf >= 0.0, e * inv1pe, inv1pe)
        if layer == 0:
            lf = log_sig
            kf = sig_neg
        else:
            p = lbp_ref[...]
            rows = [p[i:i + 1, :] for i in range(p.shape[0])]
            m = functools.reduce(jnp.maximum, rows)
            ex = [jnp.exp(r - m) for r in rows]
            lb = functools.reduce(jnp.add, ex[1:layer + 1]) / functools.reduce(jnp.add, ex)
            la = jnp.log(lb)
            lbv = jnp.log(1.0 - lb) + log_sig
            lf = jnp.maximum(la, lbv) + jnp.log(1.0 + jnp.exp(-jnp.abs(la - lbv)))
            kf = (1.0 - lb) * sig_neg

        row = lax.broadcasted_iota(jnp.int32, (t, 1), 0)
        pos = row & (HG_CHUNK - 1)
        b = lf
        sh = 1
        while sh < HG_CHUNK:
            b = b + jnp.where(pos >= sh, pltpu.roll(b, sh, axis=0), 0.0)
            sh *= 2

        n_chunks = t // HG_CHUNK
        n_heads = HG_PAIR // HG_DIM
        ti = lax.broadcasted_iota(jnp.int32, (HG_CHUNK, HG_CHUNK), 0)
        si = lax.broadcasted_iota(jnp.int32, (HG_CHUNK, HG_CHUNK), 1)

        scores = [[jnp.zeros((HG_CHUNK, HG_CHUNK), F32) for _ in range(n_heads)] for _ in range(n_chunks)]
        for h in HG_LEVELS:
            ref = _block_reference(b, h, row)
            w = jnp.exp(-jnp.abs(b - ref))
            mt = (jnp.where((row & h) != 0, qf, kf) * w).astype(BF16)
            owned = ((ti // (2 * h)) == (si // (2 * h))) & ((ti & h) != 0) & ((si & h) == 0)
            for c in range(n_chunks):
                r0 = c * HG_CHUNK
                for hh in range(n_heads):
                    c0 = hh * HG_DIM
                    m = mt[r0:r0 + HG_CHUNK, c0:c0 + HG_DIM]
                    scores[c][hh] = jnp.where(owned, _dot_nt(m, m), scores[c][hh])

        vb = zi.astype(BF16)
        qk = qf * kf
        qd = (qf * jnp.exp(b)).astype(BF16)
        ng = ng_ref[...]
        gate = _silu(zg)
        for hh in range(n_heads):
            c0 = hh * HG_DIM
            st = states[hh]
            for c in range(n_chunks):
                r0 = c * HG_CHUNK
                rs = slice(r0, r0 + HG_CHUNK)
                cs = slice(c0, c0 + HG_DIM)
                b_c = b[rs, cs]
                b_last = b_c[HG_CHUNK - 1:HG_CHUNK, :]
                v_c = vb[rs, cs]
                o = _dot(scores[c][hh].astype(BF16), v_c)
                o = o + jnp.sum(qk[rs, cs], axis=-1, keepdims=True) * zi[rs, cs]
                o = o + _dot_nt(qd[rs, cs], st.astype(BF16))
                kd = (kf[rs, cs] * jnp.exp(b_last - b_c)).astype(BF16)
                st = jnp.exp(b_last) * st + _dot_tn(v_c, kd)
                o = o * lax.rsqrt(jnp.mean(o * o, axis=-1, keepdims=True) + RMS_EPS)
                o = o * ng[:, cs] * gate[rs, cs]
                o_ref[0, pl.ds(base + r0, HG_CHUNK), cs] = o.astype(o_ref.dtype)
            states[hh] = st

    for hh, st in enumerate(states):
        state_ref[hh] = st


def _hgrn(layer, xb, w_in, lbp, ng, w_up, w_down):
    bsz, seq, d = xb.shape
    width = HG_HEADS * HG_DIM
    n_pairs = width // HG_PAIR
    ts = min(TOKEN_TILE, seq)
    n_s = seq // ts
    n_steps = bsz * n_pairs * n_s

    def slab_rows(w):
        rows = w.shape[1]
        r = next(r for r in range(BF16_ROWS, rows + 1, BF16_ROWS) if rows % r == 0 and rows // r <= n_steps)
        return r, rows // r

    step = lambda b, p, s: (b * n_pairs + p) * n_s + s

    def slab_specs(w):
        r, n_slabs = slab_rows(w)
        idx = lambda b, p, s: jnp.minimum(step(b, p, s), n_slabs - 1)
        return (pl.BlockSpec((None, r, w.shape[2]), lambda b, p, s: (layer, idx(b, p, s), 0)),
                pl.BlockSpec((r, w.shape[2]), lambda b, p, s: (idx(b, p, s), 0)))

    (up_in, up_out), (down_in, down_out) = slab_specs(w_up), slab_specs(w_down)
    wspec = lambda k: pl.BlockSpec((None, d, HG_PAIR), lambda b, p, s, k=k: (layer, 0, k * n_pairs + p))
    return pl.pallas_call(
        functools.partial(_hgrn_kernel, layer),
        grid=(bsz, n_pairs, n_s),
        in_specs=[
            pl.BlockSpec((1, ts, d), lambda b, p, s: (b, s, 0)),
            wspec(0), wspec(1), wspec(2), wspec(3),
            pl.BlockSpec((lbp.shape[0], HG_PAIR), lambda b, p, s: (0, p)),
            pl.BlockSpec((1, HG_PAIR), lambda b, p, s: (0, p)),
            up_in, down_in,
        ],
        out_specs=[pl.BlockSpec((1, ts, HG_PAIR), lambda b, p, s: (b, s, p)), up_out, down_out],
        out_shape=[jax.ShapeDtypeStruct((bsz, seq, width), BF16),
                   jax.ShapeDtypeStruct(w_up.shape[1:], BF16), jax.ShapeDtypeStruct(w_down.shape[1:], BF16)],
        scratch_shapes=[pltpu.VMEM((4, d, HG_PAIR), BF16), pltpu.VMEM((HG_PAIR // HG_DIM, HG_DIM, HG_DIM), F32)],
        compiler_params=_params(("arbitrary", "arbitrary", "arbitrary")),
        name="hgrn",
    )(xb, w_in, w_in, w_in, w_in, lbp, ng, w_up, w_down)


def _pool_kernel(x_ref, wp_ref, pw_ref, sc_ref, o_ref, w_ref, halo_ref):
    s = pl.program_id(1)

    @pl.when(jnp.logical_and(pl.program_id(0) == 0, s == 0))
    def _():
        w_ref[...] = wp_ref[...].astype(BF16)

    @pl.when(s == 0)
    def _():
        halo_ref[...] = jnp.zeros_like(halo_ref)

    x = x_ref[0].astype(BF16)
    t = x.shape[0]
    p = _dot(x, w_ref[...])
    ext = jnp.concatenate([halo_ref[...], p], axis=0)
    halo_ref[...] = p[t - POOL_HALO:, :]
    avail = s * t + lax.broadcasted_iota(jnp.int32, (t, 1), 0) + 1
    gdim = pw_ref.shape[1]
    for g, w in enumerate(POOL_WINDOWS):
        cs = slice(g * gdim, (g + 1) * gdim)
        acc = ext[:, cs]
        sh = 1
        while sh < w:
            acc = acc + pltpu.roll(acc, sh, axis=0)
            sh *= 2
        cnt = jnp.minimum(avail, w).astype(F32)
        pooled = acc[POOL_HALO:, :] * (1.0 / cnt) - p[:, cs]
        y = _dot(pooled.astype(BF16), pw_ref[g].astype(BF16)) * sc_ref[:, cs]
        o_ref[0, :, cs] = y.astype(o_ref.dtype)


def _pool(layer, xb, w_in, col_block, pw, sc):
    bsz, seq, d = xb.shape
    _, g, gdim, _ = pw.shape
    width = g * gdim
    ts = min(TOKEN_TILE, seq)
    return pl.pallas_call(
        _pool_kernel,
        grid=(bsz, seq // ts),
        in_specs=[
            pl.BlockSpec((1, ts, d), lambda b, s: (b, s, 0)),
            pl.BlockSpec((None, d, width), lambda b, s: (layer, 0, col_block), pipeline_mode=pl.Buffered(1)),
            pl.BlockSpec((None, g, gdim, gdim), lambda b, s: (layer, 0, 0, 0)),
            pl.BlockSpec((1, width), lambda b, s: (0, 0)),
        ],
        out_specs=pl.BlockSpec((1, ts, width), lambda b, s: (b, s, 0)),
        out_shape=jax.ShapeDtypeStruct((bsz, seq, width), BF16),
        scratch_shapes=[pltpu.VMEM((d, width), BF16), pltpu.VMEM((POOL_HALO, width), F32)],
        compiler_params=_params(("arbitrary", "arbitrary")),
        name="pool",
    )(xb, w_in, pw, sc)


def _sgu_kernel(x_ref, wu_ref, wv_ref, g_ref, b_ref, ws_ref, bias_ref, o_ref, w_ref):
    @pl.when(pl.program_id(0) == 0)
    def _():
        w_ref[0] = wu_ref[...].astype(BF16)
        w_ref[1] = wv_ref[...].astype(BF16)

    x = x_ref[...].astype(BF16)
    t = x.shape[0]
    zv = _dot(x, w_ref[1])
    zu = _dot(x, w_ref[0])
    vn = _layer_norm(_gelu_tanh(zv), g_ref[...], b_ref[...]).astype(BF16)
    n_chunks = t // SG_CHUNK
    gdim = vn.shape[1] // SG_GROUPS
    ri = lax.broadcasted_iota(jnp.int32, (SG_CHUNK, SG_CHUNK), 0)
    ci = lax.broadcasted_iota(jnp.int32, (SG_CHUNK, SG_CHUNK), 1)
    for g in range(SG_GROUPS):
        cs = slice(g * gdim, (g + 1) * gdim)
        w = jnp.where(ri >= ci, ws_ref[g], 0.0).astype(BF16)
        rhs = jnp.concatenate([vn[c * SG_CHUNK:(c + 1) * SG_CHUNK, cs] for c in range(n_chunks)], axis=1)
        mixed = _dot(w, rhs)
        for c in range(n_chunks):
            rs = slice(c * SG_CHUNK, (c + 1) * SG_CHUNK)
            y = _gelu_tanh(zu[rs, cs]) * (mixed[:, c * gdim:(c + 1) * gdim] + bias_ref[:, cs])
            o_ref[rs, cs] = y.astype(o_ref.dtype)


def _sgu(layer, xb2, w_in, u_block, v_block, g, b, ws, bias):
    n, d = xb2.shape
    width = g.shape[1]
    ts = min(TOKEN_TILE, n)
    const2 = lambda i: (0, 0)
    return pl.pallas_call(
        _sgu_kernel,
        grid=(n // ts,),
        in_specs=[
            pl.BlockSpec((ts, d), lambda i: (i, 0)),
            pl.BlockSpec((None, d, width), lambda i: (layer, 0, u_block), pipeline_mode=pl.Buffered(1)),
            pl.BlockSpec((None, d, width), lambda i: (layer, 0, v_block), pipeline_mode=pl.Buffered(1)),
            pl.BlockSpec((1, width), const2),
            pl.BlockSpec((1, width), const2),
            pl.BlockSpec((None,) + ws.shape[1:], lambda i: (layer, 0, 0, 0)),
            pl.BlockSpec(bias.shape, const2),
        ],
        out_specs=pl.BlockSpec((ts, width), lambda i: (i, 0)),
        out_shape=jax.ShapeDtypeStruct((n, width), BF16),
        scratch_shapes=[pltpu.VMEM((2, d, width), BF16)],
        compiler_params=_params(("arbitrary",)),
        name="sgu",
    )(xb2, w_in, w_in, g, b, ws, bias)


def _merge_kernel(x_ref, yh_ref, yp_ref, ys_ref, g0_ref, g1_ref, g2_ref, p0_ref, p1_ref, p2_ref, o_ref,
                  gw_ref, pw_ref):
    @pl.when(pl.program_id(1) == 0)
    def _():
        for k, (g_ref, p_ref) in enumerate(((g0_ref, p0_ref), (g1_ref, p1_ref), (g2_ref, p2_ref))):
            gw_ref[k] = g_ref[...].astype(BF16)
            pw_ref[k] = p_ref[...].astype(BF16)

    x = x_ref[...].astype(BF16)
    acc = None
    for k, y_ref in enumerate((yh_ref, yp_ref, ys_ref)):
        term = _sigmoid(_dot(x, gw_ref[k])) * _dot(y_ref[...], pw_ref[k])
        acc = term if acc is None else acc + term
    o_ref[...] = acc.astype(o_ref.dtype)


def _merge(layer, xb2, yh, yp, ys, w_in, gate_col0, ph, pp, ps):
    n, d = xb2.shape
    tm = min(TOKEN_TILE, n)
    tn = MERGE_TN
    gblk0 = gate_col0 // tn
    per_branch = d // tn
    width = ph.shape[1]
    assert pp.shape[1] == width and ps.shape[1] == width
    yspec = lambda y: pl.BlockSpec((tm, y.shape[1]), lambda j, i: (i, 0))
    gspec = lambda k: pl.BlockSpec((None, d, tn), lambda j, i, k=k: (layer, 0, gblk0 + k * per_branch + j))
    pspec = lambda: pl.BlockSpec((None, width, tn), lambda j, i: (layer, 0, j), pipeline_mode=pl.Buffered(1))
    return pl.pallas_call(
        _merge_kernel,
        grid=(d // tn, n // tm),
        in_specs=[pl.BlockSpec((tm, d), lambda j, i: (i, 0)), yspec(yh), yspec(yp), yspec(ys),
                  gspec(0), gspec(1), gspec(2), pspec(), pspec(), pspec()],
        out_specs=pl.BlockSpec((tm, tn), lambda j, i: (i, j)),
        out_shape=jax.ShapeDtypeStruct((n, d), BF16),
        scratch_shapes=[pltpu.VMEM((N_BRANCH, d, tn), BF16), pltpu.VMEM((N_BRANCH, width, tn), BF16)],
        compiler_params=_params(("arbitrary", "arbitrary")),
        name="merge",
    )(xb2, yh, yp, ys, w_in, w_in, w_in, ph, pp, ps)


def _outln_kernel(alpha, m_ref, w_ref, x_ref, g_ref, b_ref, o_ref, ob_ref, wb_ref, mix_ref):
    @pl.when(pl.program_id(0) == 0)
    def _():
        wb_ref[...] = w_ref[...].astype(BF16)

    mix_ref[...] = _dot(m_ref[...], wb_ref[...])
    _residual_layer_norm(alpha, x_ref, mix_ref, g_ref[...], b_ref[...], o_ref, ob_ref)


def _outln(alpha, layer, merged, w_out, x2, g, b):
    n, d = x2.shape
    tm = min(TOKEN_TILE, n)
    const2 = lambda i: (0, 0)
    row = pl.BlockSpec((tm, d), lambda i: (i, 0))
    return pl.pallas_call(
        functools.partial(_outln_kernel, alpha),
        grid=(n // tm,),
        in_specs=[row, pl.BlockSpec((None, d, d), lambda i: (layer, 0, 0), pipeline_mode=pl.Buffered(1)), row,
                  pl.BlockSpec((1, d), const2), pl.BlockSpec((1, d), const2)],
        out_specs=[row, row],
        out_shape=[jax.ShapeDtypeStruct((n, d), F32), jax.ShapeDtypeStruct((n, d), BF16)],
        scratch_shapes=[pltpu.VMEM((d, d), BF16), pltpu.VMEM((tm, d), F32)],
        compiler_params=_params(("arbitrary",)),
        name="outln",
    )(merged, w_out, x2, g, b)


def _causal_conv(h, halo, cw, cb, row):
    n = halo.shape[0]
    h1 = jnp.where(row == 0, halo[n - 1:n, :], pltpu.roll(h, 1, axis=0))
    h2 = jnp.where(row == 0, halo[n - 2:n - 1, :], jnp.where(row == 1, halo[n - 1:n, :], pltpu.roll(h, 2, axis=0)))
    return cb + cw[0:1, :] * h2 + cw[1:2, :] * h1 + cw[2:3, :] * h


def _ffn_kernel(alpha, tiles_per_seq, xb_ref, xh_ref, wa_ref, wb_ref, cwa_ref, cwb_ref, cba_ref, cbb_ref,
                wd_ref, x_ref, g_ref, b_ref, o_ref, ob_ref, acc_ref):
    i = pl.program_id(0)
    j = pl.program_id(1)

    @pl.when(j == 0)
    def _():
        acc_ref[...] = jnp.zeros_like(acc_ref)

    xb = xb_ref[...]
    xh = jnp.where(i % tiles_per_seq == 0, jnp.zeros_like(xh_ref), xh_ref[...])
    row = lax.broadcasted_iota(jnp.int32, (xb.shape[0], 1), 0)
    gates = []
    for c0 in range(0, wa_ref.shape[1], FFN_COLS):
        cs = pl.ds(c0, FFN_COLS)
        wa = wa_ref[:, cs]
        wb = wb_ref[:, cs]
        a = _causal_conv(_dot(xb, wa), _dot(xh, wa), cwa_ref[:, cs], cba_ref[:, cs], row)
        b = _causal_conv(_dot(xb, wb), _dot(xh, wb), cwb_ref[:, cs], cbb_ref[:, cs], row)
        gates.append((_silu(a) * b).astype(BF16))
    acc_ref[...] += _dot(jnp.concatenate(gates, axis=1), wd_ref[...])

    @pl.when(j == pl.num_programs(1) - 1)
    def _():
        _residual_layer_norm(alpha, x_ref, acc_ref, g_ref[...], b_ref[...], o_ref, ob_ref)


def _ffn(alpha, seq, xb2, x2, w_up, conv_w, conv_b, w_down, g, b):
    n, d = x2.shape
    f = w_down.shape[0]
    tm = min(FFN_TM, seq)
    tf = FFN_TF
    nf = f // tf
    halo_blocks = tm // BF16_ROWS
    row = pl.BlockSpec((tm, d), lambda i, j: (i, 0))
    const2 = lambda i, j: (0, 0)
    return pl.pallas_call(
        functools.partial(_ffn_kernel, alpha, seq // tm),
        grid=(n // tm, nf),
        in_specs=[
            row,
            pl.BlockSpec((BF16_ROWS, d), lambda i, j: (jnp.maximum(i * halo_blocks - 1, 0), 0)),
            pl.BlockSpec((d, tf), lambda i, j: (0, j)),
            pl.BlockSpec((d, tf), lambda i, j: (0, nf + j)),
            pl.BlockSpec((CONV_W, tf), lambda i, j: (0, j)),
            pl.BlockSpec((CONV_W, tf), lambda i, j: (0, nf + j)),
            pl.BlockSpec((1, tf), lambda i, j: (0, j)),
            pl.BlockSpec((1, tf), lambda i, j: (0, nf + j)),
            pl.BlockSpec((tf, d), lambda i, j: (j, 0)),
            row,
            pl.BlockSpec((1, d), const2),
            pl.BlockSpec((1, d), const2),
        ],
        out_specs=[row, row],
        out_shape=[jax.ShapeDtypeStruct((n, d), F32), jax.ShapeDtypeStruct((n, d), BF16)],
        scratch_shapes=[pltpu.VMEM((tm, d), F32)],
        compiler_params=_params(("parallel", "arbitrary")),
        name="ffn",
    )(xb2, xb2, w_up, w_up, conv_w, conv_w, conv_b, conv_b, w_down, x2, g, b)


def kernel(x, w_in, hg_lower_bounds, hg_norm_g, pool_w, pool_scale, sg_ln_g, sg_ln_b, sg_w, sg_b, w_hg_proj, w_pool_proj, w_sg_proj, w_out, ln1_g, ln1_b, w_up, conv_w, conv_b, w_down, ln2_g, ln2_b):
    bsz, seq, d = x.shape
    depth = w_in.shape[0]
    n = bsz * seq
    alpha = (2 * depth) ** 0.25
    hg_width = HG_HEADS * HG_DIM
    pool_width = pool_w.shape[1] * pool_w.shape[2]
    sg_width = sg_ln_g.shape[1]
    pool_col = 4 * hg_width
    u_col = pool_col + pool_width
    v_col = u_col + sg_width
    gate_col = v_col + sg_width
    assert pool_col % pool_width == 0 and u_col % sg_width == 0 and gate_col % MERGE_TN == 0

    x2 = x.reshape(n, d)
    xb2 = x2
    lbp = hg_lower_bounds.astype(F32)
    for l in range(depth):
        xb3 = xb2.reshape(bsz, seq, d)
        y_hg, w_up_b, w_down_b = _hgrn(l, xb3, w_in, lbp, hg_norm_g[l].reshape(1, hg_width), w_up, w_down)
        y_hg = y_hg.reshape(n, hg_width)
        y_pool = _pool(l, xb3, w_in, pool_col // pool_width, pool_w,
                       pool_scale[l].reshape(1, pool_width)).reshape(n, pool_width)
        gdim = sg_width // SG_GROUPS
        bias = jnp.repeat(sg_b[l].T, gdim, axis=1)
        y_sg = _sgu(l, xb2, w_in, u_col // sg_width, v_col // sg_width, sg_ln_g[l].reshape(1, sg_width),
                    sg_ln_b[l].reshape(1, sg_width), sg_w, bias)
        merged = _merge(l, xb2, y_hg, y_pool, y_sg, w_in, gate_col, w_hg_proj, w_pool_proj, w_sg_proj)
        x2, xb2 = _outln(alpha, l, merged, w_out, x2, ln1_g[l].reshape(1, d), ln1_b[l].reshape(1, d))
        x2, xb2 = _ffn(alpha, seq, xb2, x2, w_up_b, conv_w[l], conv_b[l].reshape(1, -1), w_down_b,
                       ln2_g[l].reshape(1, d), ln2_b[l].reshape(1, d))
    return x2.reshape(bsz, seq, d)
```
